```python
import jax, jax.numpy as jnp
from jax import lax
import numpy as np

D_MODEL = 1024
BATCH = 16
SEQ = 4096
DEPTH = 2
DEC_BATCH = 32
DEC_SEQ = 2048
PAST_LEN = 128

GRID_W = 64
N_MIXERS = 2
HEAD_DIM = 64
A_N_HEADS = 16
A_N_KV_HEADS = 4
A_GROUP = A_N_HEADS // A_N_KV_HEADS
A_Q_BLOCK = 128
ROPE_THETA = 10000.0
ROPE_AXIS_DIM = HEAD_DIM // 2
B_N_HEADS = 16
NA_MAX_ROWS = 8
NA_KW = 16
NA_QCOLS = 16
NA_STRIP = 2 * NA_KW
N_EXPERTS = 16
EC_CAPACITY_FACTOR = 2
EXPERT_FF = 2048
RMS_EPS = 1e-6
NEG_INF = -1e30
N_A_LAYERS = (DEPTH + N_MIXERS - 1) // N_MIXERS
N_B_LAYERS = DEPTH // N_MIXERS

kernel_name = 'hybrid_gqa_natten_ecmoe_encoder'


def rms_norm(x, g):
    x32 = x.astype(jnp.float32)
    ms = jnp.mean(x32 * x32, axis=-1, keepdims=True)
    return (x32 * lax.rsqrt(ms + RMS_EPS) * g.astype(jnp.float32)).astype(x.dtype)


def axial_rope_tables(n_tok):
    t = jnp.arange(n_tok, dtype=jnp.int32)
    pos = jnp.stack([t // GRID_W, t % GRID_W], axis=-1).astype(jnp.float32)
    inv = ROPE_THETA ** (-jnp.arange(0, ROPE_AXIS_DIM, 2, dtype=jnp.float32) / ROPE_AXIS_DIM)
    ang = pos[:, :, None] * inv
    return jnp.cos(ang), jnp.sin(ang)


def apply_axial_rope(x, cos, sin):
    xr = x.astype(jnp.float32).reshape(x.shape[:-1] + (2, ROPE_AXIS_DIM))
    half = ROPE_AXIS_DIM // 2
    x1, x2 = xr[..., :half], xr[..., half:]
    c = cos[None, :, None]
    s = sin[None, :, None]
    out = jnp.concatenate([x1 * c - x2 * s, x2 * c + x1 * s], axis=-1)
    return out.reshape(x.shape).astype(x.dtype)


def gqa_mixer(x, g, w_qkv, q_g, k_g, w_o):
    b, n, _ = x.shape
    h = rms_norm(x, g)
    qkv = h @ w_qkv
    dq = A_N_HEADS * HEAD_DIM
    dkv = A_N_KV_HEADS * HEAD_DIM
    q = qkv[..., :dq].reshape(b, n, A_N_HEADS, HEAD_DIM)
    k = qkv[..., dq:dq + dkv].reshape(b, n, A_N_KV_HEADS, HEAD_DIM)
    v = qkv[..., dq + dkv:].reshape(b, n, A_N_KV_HEADS, HEAD_DIM)
    q = rms_norm(q, q_g)
    k = rms_norm(k, k_g)
    cos, sin = axial_rope_tables(n)
    q = apply_axial_rope(q, cos, sin)
    k = apply_axial_rope(k, cos, sin)
    nblk = n // A_Q_BLOCK
    qb = q.reshape(b, nblk, A_Q_BLOCK, A_N_KV_HEADS, A_GROUP, HEAD_DIM).transpose(1, 0, 2, 3, 4, 5)
    scale = HEAD_DIM ** -0.5

    def attend(q_blk):
        s = jnp.einsum('bqkgd,bskd->bkgqs', q_blk, k).astype(jnp.float32) * scale
        p = jax.nn.softmax(s, axis=-1).astype(v.dtype)
        return jnp.einsum('bkgqs,bskd->bqkgd', p, v)

    o = lax.map(attend, qb)
    o = o.transpose(1, 0, 2, 3, 4, 5).reshape(b, n, dq)
    return o @ w_o


def na_mixer(x, g, w_qkv, q_g, k_g, rpb, w_o):
    b, n, _ = x.shape
    rows = n // GRID_W
    kr = min(NA_MAX_ROWS, rows)
    dh = B_N_HEADS * HEAD_DIM
    h = rms_norm(x, g)
    qkv = h @ w_qkv
    q = rms_norm(qkv[..., :dh].reshape(b, n, B_N_HEADS, HEAD_DIM), q_g)
    k = rms_norm(qkv[..., dh:2 * dh].reshape(b, n, B_N_HEADS, HEAD_DIM), k_g)
    v = qkv[..., 2 * dh:].reshape(b, n, B_N_HEADS, HEAD_DIM)
    kg = k.reshape(b, rows, GRID_W, B_N_HEADS, HEAD_DIM)
    vg = v.reshape(b, rows, GRID_W, B_N_HEADS, HEAD_DIM)
    ncb = GRID_W // NA_QCOLS
    qc = np.arange(GRID_W).reshape(ncb, NA_QCOLS)
    strip0 = np.clip(np.arange(ncb) * NA_QCOLS - NA_KW // 2, 0, GRID_W - NA_STRIP)
    kc = strip0[:, None] + np.arange(NA_STRIP)
    win0 = np.clip(qc - NA_KW // 2, 0, GRID_W - NA_KW)
    valid = (kc[:, None, :] >= win0[..., None]) & (kc[:, None, :] < win0[..., None] + NA_KW)
    dc = np.clip(kc[:, None, :] - qc[..., None] + NA_KW - 1, 0, 2 * NA_KW - 2)
    q_rows = q.reshape(b, rows, ncb, NA_QCOLS, B_N_HEADS, HEAD_DIM).transpose(1, 0, 2, 3, 4, 5)
    scale = HEAD_DIM ** -0.5
    rpb32 = rpb.astype(jnp.float32)

    def attend_row(args):
        r, q_blk = args
        r0 = jnp.clip(r - kr // 2, 0, rows - kr)
        k_strip = lax.dynamic_slice_in_dim(kg, r0, kr, axis=1)[:, :, kc]
        v_strip = lax.dynamic_slice_in_dim(vg, r0, kr, axis=1)[:, :, kc]
        s = jnp.einsum('bnqhd,brnkhd->bhnqrk', q_blk, k_strip).astype(jnp.float32) * scale
        dr = r0 + jnp.arange(kr, dtype=jnp.int32) - r + NA_MAX_ROWS - 1
        bias = rpb32[:, dr[:, None, None, None], dc[None]]
        s = s + bias.transpose(0, 2, 3, 1, 4)[None]
        s = jnp.where(valid[None, None, :, :, None, :], s, NEG_INF)
        p = jax.nn.softmax(s, axis=(-2, -1)).astype(v.dtype)
        return jnp.einsum('bhnqrk,brnkhd->bnqhd', p, v_strip)

    o = lax.map(attend_row, (jnp.arange(rows, dtype=jnp.int32), q_rows))
    o = o.transpose(1, 0, 2, 3, 4, 5).reshape(b, n, dh)
    return o @ w_o


def ec_moe(x, g, w_router, w_gate, w_up, w_down):
    b, n, d = x.shape
    n_tok = b * n
    cap = max(1, EC_CAPACITY_FACTOR * n_tok // N_EXPERTS)
    hf = rms_norm(x, g).reshape(n_tok, d)
    logits = (hf @ w_router).astype(jnp.float32)
    aff = jax.nn.softmax(logits, axis=-1)
    gates, idx = lax.top_k(aff.T, cap)
    xe = hf[idx]
    hid = jax.nn.silu(jnp.einsum('ecd,edf->ecf', xe, w_gate)) * jnp.einsum('ecd,edf->ecf', xe, w_up)
    out = jnp.einsum('ecf,efd->ecd', hid, w_down) * gates[..., None].astype(x.dtype)
    y = jnp.zeros((n_tok, d), x.dtype).at[idx.reshape(-1)].add(out.reshape(-1, d))
    return y.reshape(b, n, d)


def trunk(x, norm_mix, norm_ffn, a_w_qkv, a_q_norm, a_k_norm, a_w_o,
          b_w_qkv, b_q_norm, b_k_norm, b_rpb, b_w_o,
          moe_router, moe_w_gate, moe_w_up, moe_w_down):
    for i in range(DEPTH):
        j = i // N_MIXERS
        if i % N_MIXERS == 0:
            x = x + gqa_mixer(x, norm_mix[i], a_w_qkv[j], a_q_norm[j], a_k_norm[j], a_w_o[j])
        else:
            x = x + na_mixer(x, norm_mix[i], b_w_qkv[j], b_q_norm[j], b_k_norm[j], b_rpb[j], b_w_o[j])
        x = x + ec_moe(x, norm_ffn[i], moe_router[i], moe_w_gate[i], moe_w_up[i], moe_w_down[i])
    return x


def setup_inputs(seed: int = 0) -> dict:
    key = jax.random.key(seed)
    ks = jax.random.split(key, 20)
    f32 = jnp.float32
    dq_a = A_N_HEADS * HEAD_DIM
    dkv_a = A_N_KV_HEADS * HEAD_DIM
    dh_b = B_N_HEADS * HEAD_DIM
    nrm = lambda k, shape, scale: jax.random.normal(k, shape, f32) * scale
    return {
        'x_prompt': nrm(ks[0], (BATCH, SEQ, D_MODEL), 1.0),
        'x_sample': nrm(ks[1], (DEC_BATCH, DEC_SEQ, D_MODEL), 1.0),
        'norm_mix': 1.0 + nrm(ks[2], (DEPTH, D_MODEL), 0.05),
        'norm_ffn': 1.0 + nrm(ks[3], (DEPTH, D_MODEL), 0.05),
        'a_w_qkv': nrm(ks[4], (N_A_LAYERS, D_MODEL, dq_a + 2 * dkv_a), D_MODEL ** -0.5),
        'a_q_norm': 1.0 + nrm(ks[5], (N_A_LAYERS, HEAD_DIM), 0.05),
        'a_k_norm': 1.0 + nrm(ks[6], (N_A_LAYERS, HEAD_DIM), 0.05),
        'a_w_o': nrm(ks[7], (N_A_LAYERS, dq_a, D_MODEL), dq_a ** -0.5),
        'b_w_qkv': nrm(ks[8], (N_B_LAYERS, D_MODEL, 3 * dh_b), D_MODEL ** -0.5),
        'b_q_norm': 1.0 + nrm(ks[9], (N_B_LAYERS, HEAD_DIM), 0.05),
        'b_k_norm': 1.0 + nrm(ks[10], (N_B_LAYERS, HEAD_DIM), 0.05),
        'b_rpb': nrm(ks[11], (N_B_LAYERS, B_N_HEADS, 2 * NA_MAX_ROWS - 1, 2 * NA_KW - 1), 0.02),
        'b_w_o': nrm(ks[12], (N_B_LAYERS, dh_b, D_MODEL), dh_b ** -0.5),
        'moe_router': nrm(ks[13], (DEPTH, D_MODEL, N_EXPERTS), D_MODEL ** -0.5),
        'moe_w_gate': nrm(ks[14], (DEPTH, N_EXPERTS, D_MODEL, EXPERT_FF), D_MODEL ** -0.5),
        'moe_w_up': nrm(ks[15], (DEPTH, N_EXPERTS, D_MODEL, EXPERT_FF), D_MODEL ** -0.5),
        'moe_w_down': nrm(ks[16], (DEPTH, N_EXPERTS, EXPERT_FF, D_MODEL), EXPERT_FF ** -0.5),
    }


def reference(x_prompt, x_sample, norm_mix, norm_ffn, a_w_qkv, a_q_norm, a_k_norm, a_w_o,
              b_w_qkv, b_q_norm, b_k_norm, b_rpb, b_w_o,
              moe_router, moe_w_gate, moe_w_up, moe_w_down):
    y_prompt = trunk(x_prompt, norm_mix, norm_ffn, a_w_qkv, a_q_norm, a_k_norm, a_w_o,
                     b_w_qkv, b_q_norm, b_k_norm, b_rpb, b_w_o,
                     moe_router, moe_w_gate, moe_w_up, moe_w_down)
    y_sample = trunk(x_sample, norm_mix, norm_ffn, a_w_qkv, a_q_norm, a_k_norm, a_w_o,
                     b_w_qkv, b_q_norm, b_k_norm, b_rpb, b_w_o,
                     moe_router, moe_w_gate, moe_w_up, moe_w_down)
    return (y_prompt, y_sample)
```

```python
import functools

import jax
import jax.numpy as jnp
import numpy as np
from jax import lax
from jax.experimental import pallas as pl
from jax.experimental.pallas import tpu as pltpu

D_MODEL = 1024
HEAD_DIM = 64
N_HEADS = 16
A_KV_HEADS = 4
GRID_W = 64
ROPE_THETA = 10000.0
ROPE_AXIS_DIM = HEAD_DIM // 2
NA_ROWS = 8
NA_KW = 16
N_EXPERTS = 16
CAPACITY_FACTOR = 2
EXPERT_FF = 2048
RMS_EPS = 1e-6
NEG_INF = -1e30
ATTN_SCALE = HEAD_DIM ** -0.5

LANES = 128
MXU_DIM = 256
VMEM_LIMIT = 48 * 1024 * 1024

BF16 = jnp.bfloat16
F32 = jnp.float32


def _params(n_axes, vmem=VMEM_LIMIT):
    return pltpu.CompilerParams(dimension_semantics=("arbitrary",) * n_axes, vmem_limit_bytes=vmem)


def _dot(a, b):
    return jnp.dot(a, b, preferred_element_type=F32)


def _dot_nt(a, b):
    return lax.dot_general(a, b, (((1,), (1,)), ((), ())), preferred_element_type=F32)


def _rms(x, g):
    ms = jnp.mean(x * x, axis=-1, keepdims=True)
    return x * lax.rsqrt(ms + RMS_EPS) * g


def _split_bf16(x):
    hi = x.astype(BF16)
    lo = (x - hi.astype(F32)).astype(BF16)
    return hi, lo


def _head_norm(y, bd, gains):
    sq = y * y
    hi, lo = _split_bf16(sq)
    w = y.shape[1]
    parts = []
    for c in range(w // MXU_DIM):
        sl = slice(c * MXU_DIM, (c + 1) * MXU_DIM)
        parts.append(_dot(hi[:, sl], bd) + _dot(lo[:, sl], bd))
    ss = jnp.concatenate(parts, axis=1)
    return y * lax.rsqrt(ss * (1.0 / HEAD_DIM) + RMS_EPS) * gains


def _gqa_qkv_kernel(x_ref, g_ref, w_ref, bd_ref, gains_ref, cos_ref, s1_ref, s2_ref, q_ref, k_ref, v_ref):
    dq = N_HEADS * HEAD_DIM
    dkv = A_KV_HEADS * HEAD_DIM
    h = _rms(x_ref[...], g_ref[...]).astype(BF16)
    qkv = _dot(h, w_ref[...])
    qk = _head_norm(qkv[:, :dq + dkv], bd_ref[...], gains_ref[...])
    cos, s1, s2 = cos_ref[...], s1_ref[...], s2_ref[...]
    half = ROPE_AXIS_DIM // 2
    outs = []
    for c in range((dq + dkv) // LANES):
        xc = qk[:, c * LANES:(c + 1) * LANES]
        outs.append(xc * cos + pltpu.roll(xc, half, 1) * s1 + pltpu.roll(xc, LANES - half, 1) * s2)
    roped = jnp.concatenate(outs, axis=1)
    q_ref[...] = roped[:, :dq].astype(BF16)
    k_ref[...] = roped[:, dq:].astype(BF16)
    v_ref[...] = qkv[:, dq + dkv:].astype(BF16)


def _gqa_qkv(x, g, w, bd, gains, cos, s1, s2, n, tm):
    t = x.shape[0]
    dq = N_HEADS * HEAD_DIM
    dkv = A_KV_HEADS * HEAD_DIM
    npos = n // tm
    full = lambda shape: pl.BlockSpec(shape, lambda i: (0, 0))
    return pl.pallas_call(
        _gqa_qkv_kernel,
        grid=(t // tm,),
        in_specs=[
            pl.BlockSpec((tm, D_MODEL), lambda i: (i, 0)),
            full((1, D_MODEL)),
            full((D_MODEL, dq + 2 * dkv)),
            full((MXU_DIM, MXU_DIM)),
            full((1, dq + dkv)),
            pl.BlockSpec((tm, LANES), lambda i: (i % npos, 0)),
            pl.BlockSpec((tm, LANES), lambda i: (i % npos, 0)),
            pl.BlockSpec((tm, LANES), lambda i: (i % npos, 0)),
        ],
        out_specs=[
            pl.BlockSpec((tm, dq), lambda i: (i, 0)),
            pl.BlockSpec((tm, dkv), lambda i: (i, 0)),
            pl.BlockSpec((tm, dkv), lambda i: (i, 0)),
        ],
        out_shape=[
            jax.ShapeDtypeStruct((t, dq), BF16),
            jax.ShapeDtypeStruct((t, dkv), BF16),
            jax.ShapeDtypeStruct((t, dkv), BF16),
        ],
        compiler_params=_params(1),
        name="gqa_qkv",
    )(x, g, w, bd, gains, cos, s1, s2)


def _gqa_attn_kernel(q_ref, k_ref, v_ref, o_ref, *, n, tk):
    tq = q_ref.shape[0]
    lane = lax.broadcasted_iota(jnp.int32, (tq, LANES), 1)
    low = lane < HEAD_DIM
    for c in range(q_ref.shape[1] // LANES):
        qc = q_ref[:, c * LANES:(c + 1) * LANES]
        zero = jnp.zeros_like(qc)
        qa = jnp.where(low, qc, zero)
        qb = jnp.where(low, zero, qc)

        def body(j, carry):
            m_a, l_a, m_b, l_b, acc = carry
            start = pl.multiple_of(j * tk, tk)
            kt = k_ref[pl.ds(start, tk), :]
            vt = v_ref[pl.ds(start, tk), :]
            s_a = _dot_nt(qa, kt)
            s_b = _dot_nt(qb, kt)
            mn_a = jnp.maximum(m_a, jnp.max(s_a, axis=-1, keepdims=True))
            mn_b = jnp.maximum(m_b, jnp.max(s_b, axis=-1, keepdims=True))
            p_a = jnp.exp(s_a - mn_a)
            p_b = jnp.exp(s_b - mn_b)
            al_a = jnp.exp(m_a - mn_a)
            al_b = jnp.exp(m_b - mn_b)
            l_a = al_a * l_a + jnp.sum(p_a, axis=-1, keepdims=True)
            l_b = al_b * l_b + jnp.sum(p_b, axis=-1, keepdims=True)
            pv_a = _dot(p_a.astype(BF16), vt)
            pv_b = _dot(p_b.astype(BF16), vt)
            acc = acc * jnp.where(low, al_a, al_b) + jnp.where(low, pv_a, pv_b)
            return mn_a, l_a, mn_b, l_b, acc

        neg = jnp.full((tq, 1), NEG_INF, F32)
        zer = jnp.zeros((tq, 1), F32)
        _, l_a, _, l_b, acc = lax.fori_loop(0, n // tk, body, (neg, zer, neg, zer, jnp.zeros((tq, LANES), F32)))
        o_ref[:, c * LANES:(c + 1) * LANES] = (acc / jnp.where(low, l_a, l_b)).astype(BF16)


def _gqa_attn(q, k, v, b, n, tq, tk):
    t = q.shape[0]
    dq = N_HEADS * HEAD_DIM
    nq = n // tq
    pairs = A_KV_HEADS // 2
    qw = dq // pairs
    return pl.pallas_call(
        functools.partial(_gqa_attn_kernel, n=n, tk=tk),
        grid=(b, pairs, nq),
        in_specs=[
            pl.BlockSpec((tq, qw), lambda bi, p, i: (bi * nq + i, p)),
            pl.BlockSpec((n, LANES), lambda bi, p, i: (bi, p)),
            pl.BlockSpec((n, LANES), lambda bi, p, i: (bi, p)),
        ],
        out_specs=pl.BlockSpec((tq, qw), lambda bi, p, i: (bi * nq + i, p)),
        out_shape=jax.ShapeDtypeStruct((t, dq), BF16),
        compiler_params=_params(3),
        name="gqa_attn",
    )(q, k, v)


def _na_qkv_kernel(x_ref, g_ref, w_ref, bd_ref, gains_ref, q_ref, k_ref, v_ref):
    dh = N_HEADS * HEAD_DIM
    h = _rms(x_ref[...], g_ref[...]).astype(BF16)
    qkv = _dot(h, w_ref[...])
    qk = _head_norm(qkv[:, :2 * dh], bd_ref[...], gains_ref[...])
    q_ref[...] = qk[:, :dh].astype(BF16)
    k_ref[...] = qk[:, dh:].astype(BF16)
    v_ref[...] = qkv[:, 2 * dh:].astype(BF16)


def _na_qkv(x, g, w, bd, gains, tm):
    t = x.shape[0]
    dh = N_HEADS * HEAD_DIM
    full = lambda shape: pl.BlockSpec(shape, lambda i: (0, 0))
    return pl.pallas_call(
        _na_qkv_kernel,
        grid=(t // tm,),
        in_specs=[
            pl.BlockSpec((tm, D_MODEL), lambda i: (i, 0)),
            full((1, D_MODEL)),
            full((D_MODEL, 3 * dh)),
            full((MXU_DIM, MXU_DIM)),
            full((1, 2 * dh)),
        ],
        out_specs=[pl.BlockSpec((tm, dh), lambda i: (i, 0))] * 3,
        out_shape=[jax.ShapeDtypeStruct((t, dh), BF16)] * 3,
        compiler_params=_params(1),
        name="na_qkv",
    )(x, g, w, bd, gains)


def _na_attn_kernel(q_ref, k_ref, v_ref, bias_ref, o_ref, *, rows, qrows):
    rb = pl.program_id(2)
    win = NA_ROWS * GRID_W
    lane = lax.broadcasted_iota(jnp.int32, (GRID_W, LANES), 1)
    low = lane < HEAD_DIM
    for j in range(qrows):
        r = rb * qrows + j
        r0 = jnp.clip(r - NA_ROWS // 2, 0, rows - NA_ROWS)
        delta = r - r0
        start = pl.multiple_of(r0 * GRID_W, GRID_W)
        kw = k_ref[pl.ds(start, win), :]
        vw = v_ref[pl.ds(start, win), :]
        qj = q_ref[j * GRID_W:(j + 1) * GRID_W, :]
        zero = jnp.zeros_like(qj)
        halves = []
        for hh, qm in enumerate((jnp.where(low, qj, zero), jnp.where(low, zero, qj))):
            s = _dot_nt(qm, kw) + bias_ref[hh, pl.ds(delta, 1)][0]
            m = jnp.max(s, axis=-1, keepdims=True)
            p = jnp.exp(s - m)
            l = jnp.sum(p, axis=-1, keepdims=True)
            halves.append(_dot(p.astype(BF16), vw) / l)
        o_ref[j * GRID_W:(j + 1) * GRID_W, :] = jnp.where(low, halves[0], halves[1]).astype(BF16)


def _na_attn(q, k, v, bias, b, n, qrows):
    t = q.shape[0]
    dh = N_HEADS * HEAD_DIM
    rows = n // GRID_W
    nrb = rows // qrows
    tq = qrows * GRID_W
    win = NA_ROWS * GRID_W
    return pl.pallas_call(
        functools.partial(_na_attn_kernel, rows=rows, qrows=qrows),
        grid=(b, N_HEADS // 2, nrb),
        in_specs=[
            pl.BlockSpec((tq, LANES), lambda bi, p, i: (bi * nrb + i, p)),
            pl.BlockSpec((n, LANES), lambda bi, p, i: (bi, p)),
            pl.BlockSpec((n, LANES), lambda bi, p, i: (bi, p)),
            pl.BlockSpec((2, NA_ROWS, GRID_W, win), lambda bi, p, i: (p, 0, 0, 0)),
        ],
        out_specs=pl.BlockSpec((tq, LANES), lambda bi, p, i: (bi * nrb + i, p)),
        out_shape=jax.ShapeDtypeStruct((t, dh), BF16),
        compiler_params=_params(3),
        name="na_attn",
    )(q, k, v, bias)


def _na_bias_table(rpb):
    delta = np.arange(NA_ROWS)[:, None]
    krow = np.arange(NA_ROWS)[None, :]
    dr = krow - delta + NA_ROWS - 1
    qc = np.arange(GRID_W)[:, None]
    kc = np.arange(GRID_W)[None, :]
    win0 = np.clip(qc - NA_KW // 2, 0, GRID_W - NA_KW)
    valid = (kc >= win0) & (kc < win0 + NA_KW)
    dc = np.clip(kc - qc + NA_KW - 1, 0, 2 * NA_KW - 2)
    tbl = rpb.astype(F32)[:, dr[:, :, None, None], dc[None, None]]
    tbl = jnp.where(valid[None, None, None], tbl, NEG_INF)
    tbl = tbl.transpose(0, 1, 3, 2, 4)
    return tbl.reshape(N_HEADS, NA_ROWS, GRID_W, NA_ROWS * GRID_W)


def _proj_router_kernel(o_ref, w_ref, x_ref, g_ref, rhi_ref, rlo_ref, x1_ref, hf_ref, affc_ref, afft_ref):
    x1 = x_ref[...] + _dot(o_ref[...], w_ref[...])
    x1_ref[...] = x1
    hf = _rms(x1, g_ref[...])
    hi, lo = _split_bf16(hf)
    hf_ref[...] = hi
    logits = _dot(hi, rhi_ref[...]) + (_dot(hi, rlo_ref[...]) + _dot(lo, rhi_ref[...]))
    lane = lax.broadcasted_iota(jnp.int32, logits.shape, 1)
    logits = jnp.where(lane < N_EXPERTS, logits, NEG_INF)
    m = jnp.max(logits, axis=-1, keepdims=True)
    e = jnp.exp(logits - m)
    aff = e / jnp.sum(e, axis=-1, keepdims=True)
    affc_ref[...] = aff[:, :N_EXPERTS]
    afft_ref[...] = aff.T[:N_EXPERTS, :]


def _proj_router(o, w_o, x, g, r_hi, r_lo, tm):
    t = x.shape[0]
    full = lambda shape: pl.BlockSpec(shape, lambda i: (0, 0))
    return pl.pallas_call(
        _proj_router_kernel,
        grid=(t // tm,),
        in_specs=[
            pl.BlockSpec((tm, D_MODEL), lambda i: (i, 0)),
            full((D_MODEL, D_MODEL)),
            pl.BlockSpec((tm, D_MODEL), lambda i: (i, 0)),
            full((1, D_MODEL)),
            full((D_MODEL, LANES)),
            full((D_MODEL, LANES)),
        ],
        out_specs=[
            pl.BlockSpec((tm, D_MODEL), lambda i: (i, 0)),
            pl.BlockSpec((tm, D_MODEL), lambda i: (i, 0)),
            pl.BlockSpec((tm, N_EXPERTS), lambda i: (i, 0)),
            pl.BlockSpec((N_EXPERTS, tm), lambda i: (0, i)),
        ],
        out_shape=[
            jax.ShapeDtypeStruct((t, D_MODEL), F32),
            jax.ShapeDtypeStruct((t, D_MODEL), BF16),
            jax.ShapeDtypeStruct((t, N_EXPERTS), F32),
            jax.ShapeDtypeStruct((N_EXPERTS, t), F32),
        ],
        compiler_params=_params(1),
        name="proj_router",
    )(o, w_o, x, g, r_hi, r_lo)


def _select_kernel(aff_ref, upper_ref, lower_ref, pos_ref, offs_ref, *, cap):
    aff = aff_ref[...]
    ne, nr, _ = aff.shape
    bits = pltpu.bitcast(aff, jnp.int32)

    def count(mask):
        return jnp.sum(jnp.sum(jnp.where(mask, 1.0, 0.0), axis=2, keepdims=True), axis=1, keepdims=True)

    def bisect(i, thr):
        cand = thr | jnp.left_shift(jnp.int32(1), 30 - i)
        return jnp.where(count(bits >= cand) >= cap, cand, thr)

    thr = lax.fori_loop(0, 31, bisect, jnp.zeros((ne, 1, 1), jnp.int32))
    above = bits > thr
    tied = bits == thr
    need = cap - count(above)

    ones = jnp.ones((LANES, LANES), BF16)

    def prefix(mask):
        mb = jnp.where(mask, 1.0, 0.0).astype(BF16).reshape(ne * nr, LANES)
        incl = _dot(mb, upper_ref[...]).reshape(ne, nr, LANES)
        tot = _dot(mb, ones).astype(BF16).reshape(ne, nr, LANES)
        offs = jnp.stack([_dot(lower_ref[...], tot[e]) for e in range(ne)], axis=0)
        return offs + incl - mb.astype(F32).reshape(ne, nr, LANES), offs

    tie_rank, _ = prefix(tied)
    sel = above | (tied & (tie_rank < need))
    pos, offs = prefix(sel)
    pos_ref[...] = jnp.where(sel, pos, -1.0).astype(jnp.int32)
    offs_ref[...] = offs.astype(jnp.int32)


def _select(afft, cap):
    ne, n = afft.shape
    nr = n // LANES
    aff3 = afft.reshape(ne, nr, LANES)
    upper = jnp.asarray(np.triu(np.ones((LANES, LANES), np.float32)), BF16)
    lower = jnp.asarray(np.tril(np.ones((nr, nr), np.float32), -1), BF16)
    blk3 = pl.BlockSpec((ne, nr, LANES), lambda i: (0, 0, 0))
    return pl.pallas_call(
        functools.partial(_select_kernel, cap=cap),
        grid=(1,),
        in_specs=[blk3, pl.BlockSpec((LANES, LANES), lambda i: (0, 0)), pl.BlockSpec((nr, nr), lambda i: (0, 0))],
        out_specs=[blk3, blk3],
        out_shape=[jax.ShapeDtypeStruct((ne, nr, LANES), jnp.int32)] * 2,
        compiler_params=_params(1),
        name="ec_select",
    )(aff3, upper, lower)


def _pair_lists(starts, cap, bs, n_pairs):
    ne, nt = starts.shape
    nb = cap // bs
    ends = jnp.concatenate([starts[:, 1:], jnp.full((ne, 1), cap, jnp.int32)], axis=1)
    first_b = jnp.minimum(starts // bs, nb - 1)
    cnt = jnp.where(ends > starts, (ends - 1) // bs - starts // bs + 1, 0)

    def build(cnt2, first2, inner):
        cflat = cnt2.reshape(-1)
        cum = jnp.cumsum(cflat)
        p = jnp.arange(n_pairs, dtype=jnp.int32)
        valid = p < cum[-1]
        j = jnp.minimum(jnp.searchsorted(cum, p, side="right").astype(jnp.int32), cflat.shape[0] - 1)
        j = jnp.where(valid, j, jnp.max(jnp.where(cflat > 0, jnp.arange(cflat.shape[0], dtype=jnp.int32), 0)))
        within = jnp.where(valid, p - (cum[j] - cflat[j]), cflat[j] - 1)
        blk = first2.reshape(-1)[j] + within
        return j // inner, j % inner, blk.astype(jnp.int32), valid.astype(jnp.int32)

    d_e, d_t, d_b, d_v = build(cnt, first_b, nt)
    cnt_t = cnt.T.at[:, 0].max(1)
    c_t, c_e, c_b, c_v = build(cnt_t, first_b.T, ne)
    return (d_e, d_b, d_t, d_v), (c_t, c_e, c_b, c_v)


def _dispatch_kernel(pe_ref, pb_ref, pt_ref, pv_ref, hf_ref, pos_ref, xe_ref):
    p = pl.program_id(0)
    prev = jnp.maximum(p - 1, 0)
    first = (p == 0) | (pe_ref[p] != pe_ref[prev]) | (pb_ref[p] != pb_ref[prev])
    bs = xe_ref.shape[0]

    @pl.when(first)
    def _():
        xe_ref[...] = jnp.zeros_like(xe_ref)

    @pl.when(pv_ref[p] == 1)
    def _():
        rel = pos_ref[0] - pb_ref[p] * bs
        slot = lax.broadcasted_iota(jnp.int32, (bs, rel.shape[1]), 0)
        onehot = jnp.where(rel == slot, 1.0, 0.0).astype(BF16)
        xe_ref[...] += _dot(onehot, hf_ref[...]).astype(BF16)


def _dispatch(lists, hf, pos3, cap, bs, tt):
    pe, pb, pt, pv = lists
    ne = pos3.shape[0]
    nb = cap // bs
    grid_spec = pltpu.PrefetchScalarGridSpec(
        num_scalar_prefetch=4,
        grid=(pe.shape[0],),
        in_specs=[
            pl.BlockSpec((tt, D_MODEL), lambda p, e, b, t, v: (t[p], 0)),
            pl.BlockSpec((1, 1, tt), lambda p, e, b, t, v: (e[p], 0, t[p])),
        ],
        out_specs=pl.BlockSpec((bs, D_MODEL), lambda p, e, b, t, v: (e[p] * nb + b[p], 0)),
    )
    return pl.pallas_call(
        _dispatch_kernel,
        grid_spec=grid_spec,
        out_shape=jax.ShapeDtypeStruct((ne * cap, D_MODEL), BF16),
        compiler_params=_params(1),
        name="ec_dispatch",
    )(pe, pb, pt, pv, hf, pos3)


def _combine_kernel(pt_ref, pe_ref, pb_ref, pv_ref, y_ref, posc_ref, affc_ref, x_ref, out_ref):
    p = pl.program_id(0)
    prev = jnp.maximum(p - 1, 0)
    first = (p == 0) | (pt_ref[p] != pt_ref[prev])
    bs = y_ref.shape[0]

    @pl.when(first)
    def _():
        out_ref[...] = x_ref[...]

    @pl.when(pv_ref[p] == 1)
    def _():
        lane = lax.broadcasted_iota(jnp.int32, posc_ref.shape, 1)
        mine = lane == pe_ref[p]
        pos = jnp.sum(jnp.where(mine, posc_ref[...].astype(F32), 0.0), axis=1, keepdims=True)
        gate = jnp.sum(jnp.where(mine, affc_ref[...], 0.0), axis=1, keepdims=True)
        rel = (pos - (pb_ref[p] * bs).astype(F32)).astype(jnp.int32)
        slot = lax.broadcasted_iota(jnp.int32, (rel.shape[0], bs), 1)
        onehot = jnp.where(rel == slot, 1.0, 0.0).astype(BF16)
        out_ref[...] += gate * _dot(onehot, y_ref[...])


def _combine(lists, y, posc, affc, x, cap, bs, tt):
    pt, pe, pb, pv = lists
    nb = cap // bs
    t = x.shape[0]
    grid_spec = pltpu.PrefetchScalarGridSpec(
        num_scalar_prefetch=4,
        grid=(pt.shape[0],),
        in_specs=[
            pl.BlockSpec((bs, D_MODEL), lambda p, t_, e, b, v: (e[p] * nb + b[p], 0)),
            pl.BlockSpec((tt, N_EXPERTS), lambda p, t_, e, b, v: (t_[p], 0)),
            pl.BlockSpec((tt, N_EXPERTS), lambda p, t_, e, b, v: (t_[p], 0)),
            pl.BlockSpec((tt, D_MODEL), lambda p, t_, e, b, v: (t_[p], 0)),
        ],
        out_specs=pl.BlockSpec((tt, D_MODEL), lambda p, t_, e, b, v: (t_[p], 0)),
    )
    return pl.pallas_call(
        _combine_kernel,
        grid_spec=grid_spec,
        out_shape=jax.ShapeDtypeStruct((t, D_MODEL), F32),
        compiler_params=_params(1),
        name="ec_combine",
    )(pt, pe, pb, pv, y, posc, affc, x)


def _ffn_kernel(x_ref, wg_ref, wu_ref, wd_ref, y_ref, *, fc):
    x = x_ref[...]
    acc = jnp.zeros(y_ref.shape, F32)
    for c in range(EXPERT_FF // fc):
        sl = slice(c * fc, (c + 1) * fc)
        a = _dot(x, wg_ref[0, :, sl])
        u = _dot(x, wu_ref[0, :, sl])
        hid = (a * (1.0 / (1.0 + jnp.exp(-a))) * u).astype(BF16)
        acc = acc + _dot(hid, wd_ref[0, sl, :])
    y_ref[...] = acc.astype(BF16)


def _ffn(xe, wg, wu, wd, cap, bm, fc):
    ne = wg.shape[0]
    nbm = cap // bm
    return pl.pallas_call(
        functools.partial(_ffn_kernel, fc=fc),
        grid=(ne, nbm),
        in_specs=[
            pl.BlockSpec((bm, D_MODEL), lambda e, j: (e * nbm + j, 0)),
            pl.BlockSpec((1, D_MODEL, EXPERT_FF), lambda e, j: (e, 0, 0)),
            pl.BlockSpec((1, D_MODEL, EXPERT_FF), lambda e, j: (e, 0, 0)),
            pl.BlockSpec((1, EXPERT_FF, D_MODEL), lambda e, j: (e, 0, 0)),
        ],
        out_specs=pl.BlockSpec((bm, D_MODEL), lambda e, j: (e * nbm + j, 0)),
        out_shape=jax.ShapeDtypeStruct((ne * cap, D_MODEL), BF16),
        compiler_params=_params(2),
        name="ec_ffn",
    )(xe, wg, wu, wd)


def _tile(n, pref):
    return pref if n % pref == 0 else n


def _rope_tables(n):
    tok = jnp.arange(n, dtype=jnp.int32)
    pos = jnp.stack([tok // GRID_W, tok % GRID_W], axis=-1).astype(F32)
    inv = ROPE_THETA ** (-jnp.arange(0, ROPE_AXIS_DIM, 2, dtype=F32) / ROPE_AXIS_DIM)
    ang = pos[:, :, None] * inv
    cos, sin = jnp.cos(ang), jnp.sin(ang)
    half = ROPE_AXIS_DIM // 2
    cos_h = jnp.concatenate([cos, cos], axis=-1).reshape(n, HEAD_DIM)
    zeros = jnp.zeros_like(sin)
    s1_h = jnp.concatenate([zeros, sin], axis=-1).reshape(n, HEAD_DIM)
    s2_h = jnp.concatenate([-sin, zeros], axis=-1).reshape(n, HEAD_DIM)
    del half
    rep = lambda a: jnp.concatenate([a, a], axis=-1)
    return rep(cos_h), rep(s1_h), rep(s2_h)


def _block_diag_ones():
    idx = np.arange(MXU_DIM) // HEAD_DIM
    return jnp.asarray((idx[:, None] == idx[None, :]).astype(np.float32), BF16)


_A_GROUP = N_HEADS // A_KV_HEADS
_Q_PERM = np.array([kvp * 2 * _A_GROUP + half * _A_GROUP + g
                    for kvp in range(A_KV_HEADS // 2) for g in range(_A_GROUP) for half in range(2)])


def _perm_head_cols(w):
    d = w.shape[0]
    return w.reshape(d, N_HEADS, HEAD_DIM)[:, _Q_PERM].reshape(d, N_HEADS * HEAD_DIM)


def _moe(x1, hf, affc, afft, wg, wu, wd):
    n = x1.shape[0]
    cap = max(1, CAPACITY_FACTOR * n // N_EXPERTS)
    tt = _tile(n, 512)
    bs = _tile(cap, 256)
    nt, nb = n // tt, cap // bs
    pos3, offs3 = _select(afft, cap)
    starts = offs3[:, ::tt // LANES, 0]
    n_pairs = N_EXPERTS * (nt + nb) + nt
    d_lists, c_lists = _pair_lists(starts, cap, bs, n_pairs)
    posm = pos3.reshape(N_EXPERTS, n)
    xe = _dispatch(d_lists, hf, posm.reshape(N_EXPERTS, 1, n), cap, bs, tt)
    y = _ffn(xe, wg, wu, wd, cap, _tile(cap, 512), 512)
    return _combine(c_lists, y, posm.T, affc, x1, cap, bs, tt)


def _trunk(x, p):
    b, n, _ = x.shape
    t = b * n
    tm = _tile(n, 512)
    x2 = x.reshape(t, D_MODEL)
    for i in range(p["depth"]):
        j = i // 2
        g_mix = p["norm_mix"][i].reshape(1, D_MODEL)
        if i % 2 == 0:
            q, k, v = _gqa_qkv(x2, g_mix, p["a_w_qkv"][j], p["bd"], p["a_gains"][j], *p["rope"][n], n, tm)
            o = _gqa_attn(q, k, v, b, n, _tile(n, 256), _tile(n, 512))
            w_o = p["a_w_o"][j]
        else:
            q, k, v = _na_qkv(x2, g_mix, p["b_w_qkv"][j], p["bd"], p["b_gains"][j], tm)
            o = _na_attn(q, k, v, p["b_bias"][j], b, n, NA_ROWS)
            w_o = p["b_w_o"][j]
        x1, hf, affc, afft = _proj_router(o, w_o, x2, p["norm_ffn"][i].reshape(1, D_MODEL),
                                          p["r_hi"][i], p["r_lo"][i], tm)
        x2 = _moe(x1, hf, affc, afft, p["wg"][i], p["wu"][i], p["wd"][i])
    return x2.reshape(b, n, D_MODEL)


def kernel(x_prompt, x_sample, norm_mix, norm_ffn, a_w_qkv, a_q_norm, a_k_norm, a_w_o, b_w_qkv, b_q_norm, b_k_norm,
           b_rpb, b_w_o, moe_router, moe_w_gate, moe_w_up, moe_w_down):
    depth = norm_mix.shape[0]
    dq = N_HEADS * HEAD_DIM
    n_a, n_b = a_w_qkv.shape[0], b_w_qkv.shape[0]
    a_w = jnp.concatenate([jnp.stack([_perm_head_cols(a_w_qkv[j, :, :dq]) for j in range(n_a)]),
                           a_w_qkv[:, :, dq:]], axis=-1).astype(BF16)
    a_wo = jnp.stack([a_w_o[j].reshape(N_HEADS, HEAD_DIM, D_MODEL)[_Q_PERM].reshape(dq, D_MODEL)
                      for j in range(n_a)]).astype(BF16)
    a_gains = jnp.concatenate([jnp.tile(a_q_norm, (1, N_HEADS)) * ATTN_SCALE,
                               jnp.tile(a_k_norm, (1, A_KV_HEADS))], axis=-1).reshape(n_a, 1, -1)
    b_gains = jnp.concatenate([jnp.tile(b_q_norm, (1, N_HEADS)) * ATTN_SCALE,
                               jnp.tile(b_k_norm, (1, N_HEADS))], axis=-1).reshape(n_b, 1, -1)
    r_pad = jnp.pad(moe_router, ((0, 0), (0, 0), (0, LANES - N_EXPERTS)))
    r_hi = r_pad.astype(BF16)
    r_lo = (r_pad - r_hi.astype(F32)).astype(BF16)
    p = {
        "depth": depth,
        "norm_mix": norm_mix, "norm_ffn": norm_ffn,
        "a_w_qkv": a_w, "a_w_o": a_wo, "a_gains": a_gains,
        "b_w_qkv": b_w_qkv.astype(BF16), "b_w_o": b_w_o.astype(BF16), "b_gains": b_gains,
        "b_bias": jnp.stack([_na_bias_table(b_rpb[j]) for j in range(n_b)]),
        "bd": _block_diag_ones(),
        "r_hi": r_hi, "r_lo": r_lo,
        "wg": moe_w_gate.astype(BF16), "wu": moe_w_up.astype(BF16), "wd": moe_w_down.astype(BF16),
        "rope": {n: _rope_tables(n) for n in {x_prompt.shape[1], x_sample.shape[1]}},
    }
    return _trunk(x_prompt, p), _trunk(x_sample, p)
```

```python
import functools

import jax
import jax.numpy as jnp
import numpy as np
from jax import lax
from jax.experimental import pallas as pl
from jax.experimental.pallas import tpu as pltpu

D_MODEL = 1024
HEAD_DIM = 64
N_HEADS = 16
A_KV_HEADS = 4
GRID_W = 64
ROPE_THETA = 10000.0
ROPE_AXIS_DIM = HEAD_DIM // 2
NA_ROWS = 8
NA_KW = 16
NA_QB = 4
N_EXPERTS = 16
CAPACITY_FACTOR = 2
EXPERT_FF = 2048
RMS_EPS = 1e-6
NEG_INF = -1e30
ATTN_SCALE = HEAD_DIM ** -0.5
LOG2_E = float(np.log2(np.e))
Q_GAIN_SCALE = ATTN_SCALE * LOG2_E
SOFTMAX_SHIFT_LIMIT = 56.0

LANES = 128
MXU_DIM = 256
VMEM_LIMIT = 48 * 1024 * 1024
ROW_ALIGN = 16
MOE_WIN = 128
MOE_TILE = 512

BF16 = jnp.bfloat16
F32 = jnp.float32


def _params(n_axes, vmem=VMEM_LIMIT):
    return pltpu.CompilerParams(dimension_semantics=("arbitrary",) * n_axes, vmem_limit_bytes=vmem)


def _dot(a, b):
    return jnp.dot(a, b, preferred_element_type=F32)


def _dot_nt(a, b):
    return lax.dot_general(a, b, (((1,), (1,)), ((), ())), preferred_element_type=F32)


def _rms(x, g):
    ms = jnp.mean(x * x, axis=-1, keepdims=True)
    return x * lax.rsqrt(ms + RMS_EPS) * g


def _split_bf16(x):
    hi = x.astype(BF16)
    lo = (x - hi.astype(F32)).astype(BF16)
    return hi, lo


def _head_norm(y, bd, gains):
    sq = y * y
    hi, lo = _split_bf16(sq)
    w = y.shape[1]
    parts = []
    for c in range(w // MXU_DIM):
        sl = slice(c * MXU_DIM, (c + 1) * MXU_DIM)
        parts.append(_dot(hi[:, sl], bd) + _dot(lo[:, sl], bd))
    ss = jnp.concatenate(parts, axis=1)
    return y * lax.rsqrt(ss * (1.0 / HEAD_DIM) + RMS_EPS) * gains


def _gqa_qkv_kernel(x_ref, g_ref, w_ref, bd_ref, gains_ref, cos_ref, s1_ref, s2_ref, q_ref, k_ref, v_ref):
    dq = N_HEADS * HEAD_DIM
    dkv = A_KV_HEADS * HEAD_DIM
    h = _rms(x_ref[...], g_ref[...]).astype(BF16)
    qkv = _dot(h, w_ref[...])
    qk = _head_norm(qkv[:, :dq + dkv], bd_ref[...], gains_ref[...])
    cos, s1, s2 = cos_ref[...], s1_ref[...], s2_ref[...]
    half = ROPE_AXIS_DIM // 2
    outs = []
    for c in range((dq + dkv) // LANES):
        xc = qk[:, c * LANES:(c + 1) * LANES]
        outs.append(xc * cos + pltpu.roll(xc, half, 1) * s1 + pltpu.roll(xc, LANES - half, 1) * s2)
    roped = jnp.concatenate(outs, axis=1)
    q_ref[...] = roped[:, :dq].astype(BF16)
    k_ref[...] = roped[:, dq:].astype(BF16)
    v_ref[...] = qkv[:, dq + dkv:].astype(BF16)


def _gqa_qkv(x, g, w, bd, gains, cos, s1, s2, n, tm):
    t = x.shape[0]
    dq = N_HEADS * HEAD_DIM
    dkv = A_KV_HEADS * HEAD_DIM
    npos = n // tm
    full = lambda shape: pl.BlockSpec(shape, lambda i: (0, 0))
    return pl.pallas_call(
        _gqa_qkv_kernel,
        grid=(t // tm,),
        in_specs=[
            pl.BlockSpec((tm, D_MODEL), lambda i: (i, 0)),
            full((1, D_MODEL)),
            full((D_MODEL, dq + 2 * dkv)),
            full((MXU_DIM, MXU_DIM)),
            full((1, dq + dkv)),
            pl.BlockSpec((tm, LANES), lambda i: (i % npos, 0)),
            pl.BlockSpec((tm, LANES), lambda i: (i % npos, 0)),
            pl.BlockSpec((tm, LANES), lambda i: (i % npos, 0)),
        ],
        out_specs=[
            pl.BlockSpec((tm, dq), lambda i: (i, 0)),
            pl.BlockSpec((tm, dkv), lambda i: (i, 0)),
            pl.BlockSpec((tm, dkv), lambda i: (i, 0)),
        ],
        out_shape=[
            jax.ShapeDtypeStruct((t, dq), BF16),
            jax.ShapeDtypeStruct((t, dkv), BF16),
            jax.ShapeDtypeStruct((t, dkv), BF16),
        ],
        compiler_params=_params(1),
        name="gqa_qkv",
    )(x, g, w, bd, gains, cos, s1, s2)


def _gqa_attn_kernel(q_ref, k_ref, v_ref, o_ref, knorm, *, n, tk):
    tq = q_ref.shape[0]
    lane = lax.broadcasted_iota(jnp.int32, (tq, LANES), 1)
    low = lane < HEAD_DIM

    def half_sums(x, mask):
        zero = jnp.zeros_like(x)
        return (jnp.sum(jnp.where(mask, x, zero), axis=1, keepdims=True),
                jnp.sum(jnp.where(mask, zero, x), axis=1, keepdims=True))

    @pl.when(pl.program_id(2) == 0)
    def _():
        kk = k_ref[...].astype(F32)
        lo, hi = half_sums(kk * kk, lax.broadcasted_iota(jnp.int32, kk.shape, 1) < HEAD_DIM)
        knorm[0] = jnp.broadcast_to(jnp.sqrt(jnp.max(lo, axis=0, keepdims=True)), (1, LANES))
        knorm[1] = jnp.broadcast_to(jnp.sqrt(jnp.max(hi, axis=0, keepdims=True)), (1, LANES))

    def key_tile(j):
        start = pl.multiple_of(j * tk, tk)
        return k_ref[pl.ds(start, tk), :], v_ref[pl.ds(start, tk), :]

    for c in range(q_ref.shape[1] // LANES):
        cols = slice(c * LANES, (c + 1) * LANES)
        qc = q_ref[:, cols]
        zero = jnp.zeros_like(qc)
        qa = jnp.where(low, qc, zero)
        qb = jnp.where(low, zero, qc)
        qf = qc.astype(F32)
        sq_a, sq_b = half_sums(qf * qf, low)
        bnd_a = jnp.sqrt(sq_a) * knorm[0][:, :1]
        bnd_b = jnp.sqrt(sq_b) * knorm[1][:, :1]
        shift_ok = jnp.max(jnp.maximum(bnd_a, bnd_b)) <= SOFTMAX_SHIFT_LIMIT

        @pl.when(shift_ok)
        def _():
            def body(j, carry):
                ls_a, ls_b, acc = carry
                kt, vt = key_tile(j)
                p_a = jnp.exp2(_dot_nt(qa, kt) - bnd_a)
                p_b = jnp.exp2(_dot_nt(qb, kt) - bnd_b)
                for i in range(tk // LANES):
                    ls_a = ls_a + p_a[:, i * LANES:(i + 1) * LANES]
                    ls_b = ls_b + p_b[:, i * LANES:(i + 1) * LANES]
                pv_a = _dot(p_a.astype(BF16), vt)
                pv_b = _dot(p_b.astype(BF16), vt)
                return ls_a, ls_b, acc + jnp.where(low, pv_a, pv_b)

            z = jnp.zeros((tq, LANES), F32)
            ls_a, ls_b, acc = lax.fori_loop(0, n // tk, body, (z, z, z))
            l_a = jnp.sum(ls_a, axis=1, keepdims=True)
            l_b = jnp.sum(ls_b, axis=1, keepdims=True)
            o_ref[:, cols] = (acc / jnp.where(low, l_a, l_b)).astype(BF16)

        @pl.when(jnp.logical_not(shift_ok))
        def _():
            def body(j, carry):
                m_a, l_a, m_b, l_b, acc = carry
                kt, vt = key_tile(j)
                s_a = _dot_nt(qa, kt)
                s_b = _dot_nt(qb, kt)
                mn_a = jnp.maximum(m_a, jnp.max(s_a, axis=-1, keepdims=True))
                mn_b = jnp.maximum(m_b, jnp.max(s_b, axis=-1, keepdims=True))
                p_a = jnp.exp2(s_a - mn_a)
                p_b = jnp.exp2(s_b - mn_b)
                al_a = jnp.exp2(m_a - mn_a)
                al_b = jnp.exp2(m_b - mn_b)
                l_a = al_a * l_a + jnp.sum(p_a, axis=-1, keepdims=True)
                l_b = al_b * l_b + jnp.sum(p_b, axis=-1, keepdims=True)
                pv_a = _dot(p_a.astype(BF16), vt)
                pv_b = _dot(p_b.astype(BF16), vt)
                acc = acc * jnp.where(low, al_a, al_b) + jnp.where(low, pv_a, pv_b)
                return mn_a, l_a, mn_b, l_b, acc

            neg = jnp.full((tq, 1), NEG_INF, F32)
            zer = jnp.zeros((tq, 1), F32)
            _, l_a, _, l_b, acc = lax.fori_loop(0, n // tk, body, (neg, zer, neg, zer, jnp.zeros((tq, LANES), F32)))
            o_ref[:, cols] = (acc / jnp.where(low, l_a, l_b)).astype(BF16)


def _gqa_attn(q, k, v, b, n, tq, tk):
    t = q.shape[0]
    dq = N_HEADS * HEAD_DIM
    nq = n // tq
    pairs = A_KV_HEADS // 2
    qw = dq // pairs
    return pl.pallas_call(
        functools.partial(_gqa_attn_kernel, n=n, tk=tk),
        grid=(b, pairs, nq),
        in_specs=[
            pl.BlockSpec((tq, qw), lambda bi, p, i: (bi * nq + i, p)),
            pl.BlockSpec((n, LANES), lambda bi, p, i: (bi, p)),
            pl.BlockSpec((n, LANES), lambda bi, p, i: (bi, p)),
        ],
        out_specs=pl.BlockSpec((tq, qw), lambda bi, p, i: (bi * nq + i, p)),
        out_shape=jax.ShapeDtypeStruct((t, dq), BF16),
        scratch_shapes=[pltpu.VMEM((2, 1, LANES), F32)],
        compiler_params=_params(3),
        name="gqa_attn",
    )(q, k, v)


def _na_qkv_kernel(x_ref, g_ref, w_ref, bd_ref, gains_ref, q_ref, k_ref, v_ref):
    dh = N_HEADS * HEAD_DIM
    h = _rms(x_ref[...], g_ref[...]).astype(BF16)
    qkv = _dot(h, w_ref[...])
    qk = _head_norm(qkv[:, :2 * dh], bd_ref[...], gains_ref[...])
    q_ref[...] = qk[:, :dh].astype(BF16)
    k_ref[...] = qk[:, dh:].astype(BF16)
    v_ref[...] = qkv[:, 2 * dh:].astype(BF16)


def _na_qkv(x, g, w, bd, gains, tm):
    t = x.shape[0]
    dh = N_HEADS * HEAD_DIM
    full = lambda shape: pl.BlockSpec(shape, lambda i: (0, 0))
    return pl.pallas_call(
        _na_qkv_kernel,
        grid=(t // tm,),
        in_specs=[
            pl.BlockSpec((tm, D_MODEL), lambda i: (i, 0)),
            full((1, D_MODEL)),
            full((D_MODEL, 3 * dh)),
            full((MXU_DIM, MXU_DIM)),
            full((1, 2 * dh)),
        ],
        out_specs=[pl.BlockSpec((tm, dh), lambda i: (i, 0))] * 3,
        out_shape=[jax.ShapeDtypeStruct((t, dh), BF16)] * 3,
        compiler_params=_params(1),
        name="na_qkv",
    )(x, g, w, bd, gains)


def _na_attn_kernel(q_ref, k_ref, v_ref, bias_ref, o_ref, *, nblk, sub):
    ub = pl.program_id(2)
    blk = NA_QB * GRID_W
    win = 3 * blk
    lane = lax.broadcasted_iota(jnp.int32, (blk, LANES), 1)
    low = lane < HEAD_DIM
    for s in range(sub):
        u = ub * sub + s
        var = jnp.where(u == 0, 0, jnp.where(u == nblk - 1, 2, 1))
        start = pl.multiple_of(jnp.clip(u - 1, 0, nblk - 3) * blk, blk)
        kw = k_ref[pl.ds(start, win), :]
        vw = v_ref[pl.ds(start, win), :]
        qs = q_ref[s * blk:(s + 1) * blk, :]
        zero = jnp.zeros_like(qs)
        halves = []
        for hh, qm in enumerate((jnp.where(low, qs, zero), jnp.where(low, zero, qs))):
            sc = _dot_nt(qm, kw) + bias_ref[var, hh]
            m = jnp.max(sc, axis=-1, keepdims=True)
            p = jnp.exp2(sc - m)
            l = jnp.sum(p, axis=-1, keepdims=True)
            halves.append(_dot(p.astype(BF16), vw) / l)
        o_ref[s * blk:(s + 1) * blk, :] = jnp.where(low, halves[0], halves[1]).astype(BF16)


def _na_attn(q, k, v, bias, b, n, sub):
    t = q.shape[0]
    dh = N_HEADS * HEAD_DIM
    blk = NA_QB * GRID_W
    nblk = n // blk
    assert nblk >= 3 and nblk % sub == 0
    nstep = nblk // sub
    tq = sub * blk
    return pl.pallas_call(
        functools.partial(_na_attn_kernel, nblk=nblk, sub=sub),
        grid=(N_HEADS // 2, b, nstep),
        in_specs=[
            pl.BlockSpec((tq, LANES), lambda p, bi, i: (bi * nstep + i, p)),
            pl.BlockSpec((n, LANES), lambda p, bi, i: (bi, p)),
            pl.BlockSpec((n, LANES), lambda p, bi, i: (bi, p)),
            pl.BlockSpec((3, 2, blk, 3 * blk), lambda p, bi, i: (0, p, 0, 0)),
        ],
        out_specs=pl.BlockSpec((tq, LANES), lambda p, bi, i: (bi * nstep + i, p)),
        out_shape=jax.ShapeDtypeStruct((t, dh), BF16),
        compiler_params=_params(3),
        name="na_attn",
    )(q, k, v, bias)


def _na_bias_table(rpb):
    nkr = 3 * NA_QB
    i = np.arange(NA_QB)[:, None]
    kr = np.arange(nkr)[None, :]
    variants = [(0 * i, NA_ROWS - 1), (i, NA_ROWS - 1 - NA_QB), (NA_QB + 0 * i, NA_ROWS - 1 - 2 * NA_QB)]
    row_valid = np.stack([(kr >= r0) & (kr < r0 + NA_ROWS) for r0, _ in variants])
    dr = np.stack([np.clip(kr - i + off, 0, 2 * NA_ROWS - 2) for _, off in variants])
    row_sel = (dr[..., None] == np.arange(2 * NA_ROWS - 1)).astype(np.float32)
    qc = np.arange(GRID_W)[:, None]
    kc = np.arange(GRID_W)[None, :]
    win0 = np.clip(qc - NA_KW // 2, 0, GRID_W - NA_KW)
    col_valid = (kc >= win0) & (kc < win0 + NA_KW)
    dc = np.clip(kc - qc + NA_KW - 1, 0, 2 * NA_KW - 2)
    col_sel = (dc[..., None] == np.arange(2 * NA_KW - 1)).astype(np.float32)
    hp = lax.Precision.HIGHEST
    tmp = jnp.einsum("vikd,hde->vhike", row_sel, rpb.astype(F32), precision=hp)
    tbl = jnp.einsum("vhike,qce->vhiqkc", tmp, col_sel, precision=hp)
    valid = row_valid[:, :, None, :, None] & col_valid[None, None, :, None, :]
    tbl = jnp.where(valid[:, None], tbl * LOG2_E, NEG_INF)
    return tbl.reshape(3, N_HEADS, NA_QB * GRID_W, nkr * GRID_W)


def _proj_router_kernel(o_ref, w_ref, x_ref, g_ref, rhi_ref, rlo_ref, x1_ref, hf_ref, affc_ref, afft_ref):
    x1 = x_ref[...] + _dot(o_ref[...], w_ref[...])
    x1_ref[...] = x1
    hf = _rms(x1, g_ref[...])
    hi, lo = _split_bf16(hf)
    hf_ref[...] = hi
    logits = _dot(hi, rhi_ref[...]) + (_dot(hi, rlo_ref[...]) + _dot(lo, rhi_ref[...]))
    lane = lax.broadcasted_iota(jnp.int32, logits.shape, 1)
    logits = jnp.where(lane < N_EXPERTS, logits, NEG_INF)
    m = jnp.max(logits, axis=-1, keepdims=True)
    e = jnp.exp(logits - m)
    aff = e / jnp.sum(e, axis=-1, keepdims=True)
    affc_ref[...] = aff[:, :N_EXPERTS]
    afft_ref[...] = aff.T[:N_EXPERTS, :]


def _proj_router(o, w_o, x, g, r_hi, r_lo, tm):
    t = x.shape[0]
    full = lambda shape: pl.BlockSpec(shape, lambda i: (0, 0))
    return pl.pallas_call(
        _proj_router_kernel,
        grid=(t // tm,),
        in_specs=[
            pl.BlockSpec((tm, D_MODEL), lambda i: (i, 0)),
            full((D_MODEL, D_MODEL)),
            pl.BlockSpec((tm, D_MODEL), lambda i: (i, 0)),
            full((1, D_MODEL)),
            full((D_MODEL, LANES)),
            full((D_MODEL, LANES)),
        ],
        out_specs=[
            pl.BlockSpec((tm, D_MODEL), lambda i: (i, 0)),
            pl.BlockSpec((tm, D_MODEL), lambda i: (i, 0)),
            pl.BlockSpec((tm, N_EXPERTS), lambda i: (i, 0)),
            pl.BlockSpec((N_EXPERTS, tm), lambda i: (0, i)),
        ],
        out_shape=[
            jax.ShapeDtypeStruct((t, D_MODEL), F32),
            jax.ShapeDtypeStruct((t, D_MODEL), BF16),
            jax.ShapeDtypeStruct((t, N_EXPERTS), F32),
            jax.ShapeDtypeStruct((N_EXPERTS, t), F32),
        ],
        compiler_params=_params(1),
        name="proj_router",
    )(o, w_o, x, g, r_hi, r_lo)


def _select_kernel(aff_ref, upper_ref, lower_ref, pos_ref, offs_ref, *, cap):
    aff = aff_ref[...]
    ne, nr, _ = aff.shape
    bits = pltpu.bitcast(aff, jnp.int32)

    def count(mask):
        return jnp.sum(jnp.sum(jnp.where(mask, 1.0, 0.0), axis=2, keepdims=True), axis=1, keepdims=True)

    def bisect(i, thr):
        cand = thr | jnp.left_shift(jnp.int32(1), 30 - i)
        return jnp.where(count(bits >= cand) >= cap, cand, thr)

    thr = lax.fori_loop(0, 31, bisect, jnp.zeros((ne, 1, 1), jnp.int32))
    above = bits > thr
    tied = bits == thr
    need = cap - count(above)

    ones = jnp.ones((LANES, LANES), BF16)

    def prefix(mask):
        mb = jnp.where(mask, 1.0, 0.0).astype(BF16).reshape(ne * nr, LANES)
        incl = _dot(mb, upper_ref[...]).reshape(ne, nr, LANES)
        tot = _dot(mb, ones).astype(BF16).reshape(ne, nr, LANES)
        offs = jnp.stack([_dot(lower_ref[...], tot[e]) for e in range(ne)], axis=0)
        return offs + incl - mb.astype(F32).reshape(ne, nr, LANES), offs

    tie_rank, _ = prefix(tied)
    sel = above | (tied & (tie_rank < need))
    pos, offs = prefix(sel)
    pos_ref[...] = jnp.where(sel, pos, -1.0).astype(jnp.int32)
    offs_ref[...] = offs.astype(jnp.int32)


def _select(afft, cap):
    ne, n = afft.shape
    nr = n // LANES
    aff3 = afft.reshape(ne, nr, LANES)
    upper = jnp.asarray(np.triu(np.ones((LANES, LANES), np.float32)), BF16)
    lower = jnp.asarray(np.tril(np.ones((nr, nr), np.float32), -1), BF16)
    blk3 = pl.BlockSpec((ne, nr, LANES), lambda i: (0, 0, 0))
    return pl.pallas_call(
        functools.partial(_select_kernel, cap=cap),
        grid=(1,),
        in_specs=[blk3, pl.BlockSpec((LANES, LANES), lambda i: (0, 0)), pl.BlockSpec((nr, nr), lambda i: (0, 0))],
        out_specs=[blk3, blk3],
        out_shape=[jax.ShapeDtypeStruct((ne, nr, LANES), jnp.int32)] * 2,
        compiler_params=_params(1),
        name="ec_select",
    )(aff3, upper, lower)


def _align_down(x):
    return lax.shift_left(lax.shift_right_logical(x, ROW_ALIGN.bit_length() - 1), ROW_ALIGN.bit_length() - 1)


def _window_info(st_ref, e, t, nt):
    start = st_ref[e * (nt + 1) + t]
    end = st_ref[e * (nt + 1) + t + 1]
    base = _align_down(start)
    return base, end, lax.shift_right_logical(end - base, MOE_WIN.bit_length() - 1) + 1


def _dispatch_kernel(st_ref, hf_ref, pos_ref, xe_ref, stage, extra, carry, sems, xsem, *, nt, cap):
    t = pl.program_id(0)
    ne, tt = pos_ref.shape
    slot = t % 2
    hf = hf_ref[...]
    rows = lax.broadcasted_iota(jnp.int32, (MOE_WIN, tt), 0)
    info = [_window_info(st_ref, e, t, nt) for e in range(ne)]

    @pl.when(t == 0)
    def _():
        carry[...] = jnp.zeros_like(carry)
        extra[...] = jnp.zeros_like(extra)
        pads = [pltpu.make_async_copy(extra.at[pl.ds(e * MOE_WIN, MOE_WIN), :], xe_ref.at[e, pl.ds(cap, MOE_WIN), :], xsem)
                for e in range(ne)]
        for cp in pads:
            cp.start()
        for cp in pads:
            cp.wait()

    def gathered(k):
        parts = []
        for e in range(ne):
            rel = pos_ref[e:e + 1, :] - (info[e][0] + k * MOE_WIN)
            parts.append(jnp.where(rel == rows, 1.0, 0.0).astype(BF16))
        return _dot(jnp.concatenate(parts, axis=0), hf).astype(BF16)

    def window(buf, e, k, sem):
        row = pl.multiple_of(info[e][0] + k * MOE_WIN, ROW_ALIGN)
        return pltpu.make_async_copy(buf.at[pl.ds(e * MOE_WIN, MOE_WIN), :], xe_ref.at[e, pl.ds(row, MOE_WIN), :], sem)

    stage[slot] = gathered(0)
    for e in range(ne):
        base, end, nwin = info[e]
        head = pl.ds(e * MOE_WIN, ROW_ALIGN)
        stage[slot, head, :] = stage[slot, head, :] + carry[e]

        @pl.when(nwin == 1)
        def _():
            tail = pl.multiple_of(e * MOE_WIN + _align_down(end) - base, ROW_ALIGN)
            carry[e] = stage[slot, pl.ds(tail, ROW_ALIGN), :]

        @pl.when(t > 0)
        def _():
            window(stage.at[1 - slot], e, 0, sems.at[e]).wait()

        window(stage.at[slot], e, 0, sems.at[e]).start()

    def more(k, c):
        extra[...] = gathered(k)
        for e in range(ne):
            base, end, nwin = info[e]

            @pl.when(k < nwin)
            def _():
                cp = window(extra, e, k, xsem)
                cp.start()
                cp.wait()

            @pl.when(k == nwin - 1)
            def _():
                tail = pl.multiple_of(e * MOE_WIN + _align_down(end) - base - k * MOE_WIN, ROW_ALIGN)
                carry[e] = extra[pl.ds(tail, ROW_ALIGN), :]
        return c

    lax.fori_loop(1, functools.reduce(jnp.maximum, [i[2] for i in info]), more, 0)

    @pl.when(t == nt - 1)
    def _():
        for e in range(ne):
            window(stage.at[slot], e, 0, sems.at[e]).wait()


def _dispatch(starts, hf, posm, cap, tt):
    ne, n = posm.shape
    nt = n // tt
    grid_spec = pltpu.PrefetchScalarGridSpec(
        num_scalar_prefetch=1,
        grid=(nt,),
        in_specs=[
            pl.BlockSpec((tt, D_MODEL), lambda t, st: (t, 0)),
            pl.BlockSpec((ne, tt), lambda t, st: (0, t)),
        ],
        out_specs=pl.BlockSpec(memory_space=pl.ANY),
        scratch_shapes=[
            pltpu.VMEM((2, ne * MOE_WIN, D_MODEL), BF16),
            pltpu.VMEM((ne * MOE_WIN, D_MODEL), BF16),
            pltpu.VMEM((ne, ROW_ALIGN, D_MODEL), BF16),
            pltpu.SemaphoreType.DMA((ne,)),
            pltpu.SemaphoreType.DMA(()),
        ],
    )
    return pl.pallas_call(
        functools.partial(_dispatch_kernel, nt=nt, cap=cap),
        grid_spec=grid_spec,
        out_shape=jax.ShapeDtypeStruct((ne, cap + MOE_WIN, D_MODEL), BF16),
        compiler_params=_params(1),
        name="ec_dispatch",
    )(starts, hf, posm)


def _combine_kernel(st_ref, y_ref, posc_ref, affc_ref, x_ref, out_ref, ybuf, extra, sems, xsem, *, nt):
    t = pl.program_id(0)
    tt, ne = posc_ref.shape
    slot = t % 2
    lanes = lax.broadcasted_iota(jnp.int32, (tt, MOE_WIN), 1)

    def window(tile, e, k, buf, sem):
        row = pl.multiple_of(_window_info(st_ref, e, tile, nt)[0] + k * MOE_WIN, ROW_ALIGN)
        return pltpu.make_async_copy(y_ref.at[e, pl.ds(row, MOE_WIN), :], buf.at[pl.ds(e * MOE_WIN, MOE_WIN), :], sem)

    @pl.when(t == 0)
    def _():
        extra[...] = jnp.zeros_like(extra)
        for e in range(ne):
            window(0, e, 0, ybuf.at[0], sems.at[0, e]).start()

    @pl.when(t + 1 < nt)
    def _():
        for e in range(ne):
            window(t + 1, e, 0, ybuf.at[1 - slot], sems.at[1 - slot, e]).start()

    for e in range(ne):
        window(t, e, 0, ybuf.at[slot], sems.at[slot, e]).wait()

    info = [_window_info(st_ref, e, t, nt) for e in range(ne)]

    def gates(k):
        his, los = [], []
        for e in range(ne):
            rel = posc_ref[:, e:e + 1] - (info[e][0] + k * MOE_WIN)
            hi, lo = _split_bf16(jnp.where(rel == lanes, affc_ref[:, e:e + 1], 0.0))
            his.append(hi)
            los.append(lo)
        return jnp.concatenate(his, axis=1), jnp.concatenate(los, axis=1)

    hi, lo = gates(0)
    y = ybuf[slot]
    out_ref[...] = x_ref[...] + (_dot(hi, y) + _dot(lo, y))

    def more(k, c):
        for e in range(ne):
            @pl.when(k < info[e][2])
            def _():
                cp = window(t, e, k, extra, xsem)
                cp.start()
                cp.wait()
        hi, lo = gates(k)
        y = extra[...]
        out_ref[...] += _dot(hi, y) + _dot(lo, y)
        return c

    lax.fori_loop(1, functools.reduce(jnp.maximum, [i[2] for i in info]), more, 0)


def _combine(starts, y, posc, affc, x, tt):
    n, ne = posc.shape
    nt = n // tt
    grid_spec = pltpu.PrefetchScalarGridSpec(
        num_scalar_prefetch=1,
        grid=(nt,),
        in_specs=[
            pl.BlockSpec(memory_space=pl.ANY),
            pl.BlockSpec((tt, ne), lambda t, st: (t, 0)),
            pl.BlockSpec((tt, ne), lambda t, st: (t, 0)),
            pl.BlockSpec((tt, D_MODEL), lambda t, st: (t, 0)),
        ],
        out_specs=pl.BlockSpec((tt, D_MODEL), lambda t, st: (t, 0)),
        scratch_shapes=[
            pltpu.VMEM((2, ne * MOE_WIN, D_MODEL), BF16),
            pltpu.VMEM((ne * MOE_WIN, D_MODEL), BF16),
            pltpu.SemaphoreType.DMA((2, ne)),
            pltpu.SemaphoreType.DMA(()),
        ],
    )
    return pl.pallas_call(
        functools.partial(_combine_kernel, nt=nt),
        grid_spec=grid_spec,
        out_shape=jax.ShapeDtypeStruct((n, D_MODEL), F32),
        compiler_params=_params(1),
        name="ec_combine",
    )(starts, y, posc, affc, x)


def _ffn_kernel(x_ref, wg_ref, wu_ref, wd_ref, y_ref, *, fc, nbm):
    j = pl.program_id(1)

    @pl.when(j < nbm)
    def _():
        x = x_ref[0]
        acc = jnp.zeros(y_ref.shape[1:], F32)
        for c in range(EXPERT_FF // fc):
            sl = slice(c * fc, (c + 1) * fc)
            a = _dot(x, wg_ref[0, :, sl])
            u = _dot(x, wu_ref[0, :, sl])
            hid = (a * (1.0 / (1.0 + jnp.exp(-a))) * u).astype(BF16)
            acc = acc + _dot(hid, wd_ref[0, sl, :])
        y_ref[0] = acc.astype(BF16)

    @pl.when(j == nbm)
    def _():
        y_ref[...] = jnp.zeros_like(y_ref)


def _ffn(xe, wg, wu, wd, cap, bm, fc):
    ne = wg.shape[0]
    nbm = cap // bm
    return pl.pallas_call(
        functools.partial(_ffn_kernel, fc=fc, nbm=nbm),
        grid=(ne, nbm + 1),
        in_specs=[
            pl.BlockSpec((1, bm, D_MODEL), lambda e, j: (e, jnp.minimum(j, nbm - 1), 0)),
            pl.BlockSpec((1, D_MODEL, EXPERT_FF), lambda e, j: (e, 0, 0)),
            pl.BlockSpec((1, D_MODEL, EXPERT_FF), lambda e, j: (e, 0, 0)),
            pl.BlockSpec((1, EXPERT_FF, D_MODEL), lambda e, j: (e, 0, 0)),
        ],
        out_specs=pl.BlockSpec((1, bm, D_MODEL), lambda e, j: (e, j, 0)),
        out_shape=jax.ShapeDtypeStruct((ne, cap + bm, D_MODEL), BF16),
        compiler_params=_params(2),
        name="ec_ffn",
    )(xe, wg, wu, wd)


def _tile(n, pref):
    return pref if n % pref == 0 else n


def _rope_tables(n):
    tok = jnp.arange(n, dtype=jnp.int32)
    pos = jnp.stack([tok // GRID_W, tok % GRID_W], axis=-1).astype(F32)
    inv = ROPE_THETA ** (-jnp.arange(0, ROPE_AXIS_DIM, 2, dtype=F32) / ROPE_AXIS_DIM)
    ang = pos[:, :, None] * inv
    cos, sin = jnp.cos(ang), jnp.sin(ang)
    half = ROPE_AXIS_DIM // 2
    cos_h = jnp.concatenate([cos, cos], axis=-1).reshape(n, HEAD_DIM)
    zeros = jnp.zeros_like(sin)
    s1_h = jnp.concatenate([zeros, sin], axis=-1).reshape(n, HEAD_DIM)
    s2_h = jnp.concatenate([-sin, zeros], axis=-1).reshape(n, HEAD_DIM)
    del half
    rep = lambda a: jnp.concatenate([a, a], axis=-1)
    return rep(cos_h), rep(s1_h), rep(s2_h)


def _block_diag_ones():
    idx = np.arange(MXU_DIM) // HEAD_DIM
    return jnp.asarray((idx[:, None] == idx[None, :]).astype(np.float32), BF16)


_A_GROUP = N_HEADS // A_KV_HEADS
_Q_PERM = np.array([kvp * 2 * _A_GROUP + half * _A_GROUP + g
                    for kvp in range(A_KV_HEADS // 2) for g in range(_A_GROUP) for half in range(2)])


def _perm_head_cols(w):
    d = w.shape[0]
    return w.reshape(d, N_HEADS, HEAD_DIM)[:, _Q_PERM].reshape(d, N_HEADS * HEAD_DIM)


def _moe(x1, hf, affc, afft, wg, wu, wd):
    n = x1.shape[0]
    cap = max(1, CAPACITY_FACTOR * n // N_EXPERTS)
    tt = _tile(n, MOE_TILE)
    bm = _tile(cap, 512)
    pos3, offs3 = _select(afft, cap)
    starts = jnp.concatenate([offs3[:, ::tt // LANES, 0], jnp.full((N_EXPERTS, 1), cap, jnp.int32)], axis=1)
    starts = starts.reshape(-1)
    posm = pos3.reshape(N_EXPERTS, n)
    xe = _dispatch(starts, hf, posm, cap, tt)
    y = _ffn(xe, wg, wu, wd, cap, bm, 512)
    return _combine(starts, y, posm.T, affc, x1, tt)


def _trunk(x, p):
    b, n, _ = x.shape
    t = b * n
    tm = _tile(n, 512)
    x2 = x.reshape(t, D_MODEL)
    for i in range(p["depth"]):
        j = i // 2
        g_mix = p["norm_mix"][i].reshape(1, D_MODEL)
        if i % 2 == 0:
            q, k, v = _gqa_qkv(x2, g_mix, p["a_w_qkv"][j], p["bd"], p["a_gains"][j], *p["rope"][n], n, tm)
            o = _gqa_attn(q, k, v, b, n, _tile(n, 256), _tile(n, 512))
            w_o = p["a_w_o"][j]
        else:
            q, k, v = _na_qkv(x2, g_mix, p["b_w_qkv"][j], p["bd"], p["b_gains"][j], tm)
            o = _na_attn(q, k, v, p["b_bias"][j], b, n, 2)
            w_o = p["b_w_o"][j]
        x1, hf, affc, afft = _proj_router(o, w_o, x2, p["norm_ffn"][i].reshape(1, D_MODEL),
                                          p["r_hi"][i], p["r_lo"][i], tm)
        x2 = _moe(x1, hf, affc, afft, p["wg"][i], p["wu"][i], p["wd"][i])
    return x2.reshape(b, n, D_MODEL)


def kernel(x_prompt, x_sample, norm_mix, norm_ffn, a_w_qkv, a_q_norm, a_k_norm, a_w_o, b_w_qkv, b_q_norm, b_k_norm,
           b_rpb, b_w_o, moe_router, moe_w_gate, moe_w_up, moe_w_down):
    depth = norm_mix.shape[0]
    dq = N_HEADS * HEAD_DIM
    n_a, n_b = a_w_qkv.shape[0], b_w_qkv.shape[0]
    a_w = jnp.concatenate([jnp.stack([_perm_head_cols(a_w_qkv[j, :, :dq]) for j in range(n_a)]),
                           a_w_qkv[:, :, dq:]], axis=-1).astype(BF16)
    a_wo = jnp.stack([a_w_o[j].reshape(N_HEADS, HEAD_DIM, D_MODEL)[_Q_PERM].reshape(dq, D_MODEL)
                      for j in range(n_a)]).astype(BF16)
    a_gains = jnp.concatenate([jnp.tile(a_q_norm, (1, N_HEADS)) * Q_GAIN_SCALE,
                               jnp.tile(a_k_norm, (1, A_KV_HEADS))], axis=-1).reshape(n_a, 1, -1)
    b_gains = jnp.concatenate([jnp.tile(b_q_norm, (1, N_HEADS)) * Q_GAIN_SCALE,
                               jnp.tile(b_k_norm, (1, N_HEADS))], axis=-1).reshape(n_b, 1, -1)
    r_pad = jnp.pad(moe_router, ((0, 0), (0, 0), (0, LANES - N_EXPERTS)))
    r_hi = r_pad.astype(BF16)
    r_lo = (r_pad - r_hi.astype(F32)).astype(BF16)
    p = {
        "depth": depth,
        "norm_mix": norm_mix, "norm_ffn": norm_ffn,
        "a_w_qkv": a_w, "a_w_o": a_wo, "a_gains": a_gains,
        "b_w_qkv": b_w_qkv.astype(BF16), "b_w_o": b_w_o.astype(BF16), "b_gains": b_gains,
        "b_bias": jnp.stack([_na_bias_table(b_rpb[j]) for j in range(n_b)]),
        "bd": _block_diag_ones(),
        "r_hi": r_hi, "r_lo": r_lo,
        "wg": moe_w_gate.astype(BF16), "wu": moe_w_up.astype(BF16), "wd": moe_w_down.astype(BF16),
        "rope": {n: _rope_tables(n) for n in {x_prompt.shape[1], x_sample.shape[1]}},
    }
    return _trunk(x_prompt, p), _trunk(x_sample, p)
```

```python
import functools

import jax
import jax.numpy as jnp
import numpy as np
from jax import lax
from jax.experimental import pallas as pl
from jax.experimental.pallas import tpu as pltpu

D_MODEL = 1024
HEAD_DIM = 64
N_HEADS = 16
A_KV_HEADS = 4
GRID_W = 64
ROPE_THETA = 10000.0
ROPE_AXIS_DIM = HEAD_DIM // 2
NA_ROWS = 8
NA_KW = 16
NA_QB = 4
N_EXPERTS = 16
CAPACITY_FACTOR = 2
EXPERT_FF = 2048
RMS_EPS = 1e-6
NEG_INF = -1e30
ATTN_SCALE = HEAD_DIM ** -0.5
LOG2_E = float(np.log2(np.e))
Q_GAIN_SCALE = ATTN_SCALE * LOG2_E
SOFTMAX_SHIFT_LIMIT = 56.0
GQA_UNROLL = 4

LANES = 128
MXU_DIM = 256
VMEM_LIMIT = 48 * 1024 * 1024
ROW_ALIGN = 16
MOE_WIN = 128
MOE_TILE = 512

BF16 = jnp.bfloat16
F32 = jnp.float32


def _params(n_axes, vmem=VMEM_LIMIT):
    return pltpu.CompilerParams(dimension_semantics=("arbitrary",) * n_axes, vmem_limit_bytes=vmem)


def _dot(a, b):
    return jnp.dot(a, b, preferred_element_type=F32)


def _dot_nt(a, b):
    return lax.dot_general(a, b, (((1,), (1,)), ((), ())), preferred_element_type=F32)


def _rms(x, g):
    ms = jnp.mean(x * x, axis=-1, keepdims=True)
    return x * lax.rsqrt(ms + RMS_EPS) * g


def _split_bf16(x):
    hi = x.astype(BF16)
    lo = (x - hi.astype(F32)).astype(BF16)
    return hi, lo


def _head_norm(y, bd, gains):
    sq = y * y
    hi, lo = _split_bf16(sq)
    w = y.shape[1]
    parts = []
    for c in range(w // MXU_DIM):
        sl = slice(c * MXU_DIM, (c + 1) * MXU_DIM)
        parts.append(_dot(hi[:, sl], bd) + _dot(lo[:, sl], bd))
    ss = jnp.concatenate(parts, axis=1)
    return y * lax.rsqrt(ss * (1.0 / HEAD_DIM) + RMS_EPS) * gains


def _gqa_qkv_kernel(x_ref, g_ref, w_ref, bd_ref, gains_ref, cos_ref, s1_ref, s2_ref, q_ref, k_ref, v_ref):
    dq = N_HEADS * HEAD_DIM
    dkv = A_KV_HEADS * HEAD_DIM
    h = _rms(x_ref[...], g_ref[...]).astype(BF16)
    qkv = _dot(h, w_ref[...])
    qk = _head_norm(qkv[:, :dq + dkv], bd_ref[...], gains_ref[...])
    cos, s1, s2 = cos_ref[...], s1_ref[...], s2_ref[...]
    half = ROPE_AXIS_DIM // 2
    outs = []
    for c in range((dq + dkv) // LANES):
        xc = qk[:, c * LANES:(c + 1) * LANES]
        outs.append(xc * cos + pltpu.roll(xc, half, 1) * s1 + pltpu.roll(xc, LANES - half, 1) * s2)
    roped = jnp.concatenate(outs, axis=1)
    q_ref[...] = roped[:, :dq].astype(BF16)
    k_ref[...] = roped[:, dq:].astype(BF16)
    v_ref[...] = qkv[:, dq + dkv:].astype(BF16)


def _gqa_qkv(x, g, w, bd, gains, cos, s1, s2, n, tm):
    t = x.shape[0]
    dq = N_HEADS * HEAD_DIM
    dkv = A_KV_HEADS * HEAD_DIM
    npos = n // tm
    full = lambda shape: pl.BlockSpec(shape, lambda i: (0, 0))
    return pl.pallas_call(
        _gqa_qkv_kernel,
        grid=(t // tm,),
        in_specs=[
            pl.BlockSpec((tm, D_MODEL), lambda i: (i, 0)),
            full((1, D_MODEL)),
            full((D_MODEL, dq + 2 * dkv)),
            full((MXU_DIM, MXU_DIM)),
            full((1, dq + dkv)),
            pl.BlockSpec((tm, LANES), lambda i: (i % npos, 0)),
            pl.BlockSpec((tm, LANES), lambda i: (i % npos, 0)),
            pl.BlockSpec((tm, LANES), lambda i: (i % npos, 0)),
        ],
        out_specs=[
            pl.BlockSpec((tm, dq), lambda i: (i, 0)),
            pl.BlockSpec((tm, dkv), lambda i: (i, 0)),
            pl.BlockSpec((tm, dkv), lambda i: (i, 0)),
        ],
        out_shape=[
            jax.ShapeDtypeStruct((t, dq), BF16),
            jax.ShapeDtypeStruct((t, dkv), BF16),
            jax.ShapeDtypeStruct((t, dkv), BF16),
        ],
        compiler_params=_params(1),
        name="gqa_qkv",
    )(x, g, w, bd, gains, cos, s1, s2)


def _gqa_attn_kernel(bound_ref, q_ref, k_ref, v_ref, o_ref, *, n, tk, unroll):
    tq = q_ref.shape[0]
    lane = lax.broadcasted_iota(jnp.int32, (tq, LANES), 1)
    low = lane < HEAD_DIM
    bound = bound_ref[0]

    def key_tile(j):
        start = pl.multiple_of(j * tk, tk)
        return k_ref[pl.ds(start, tk), :], v_ref[pl.ds(start, tk), :]

    for c in range(q_ref.shape[1] // LANES):
        cols = slice(c * LANES, (c + 1) * LANES)
        qc = q_ref[:, cols]
        zero = jnp.zeros_like(qc)
        qa = jnp.where(low, qc, zero)
        qb = jnp.where(low, zero, qc)

        @pl.when(bound <= SOFTMAX_SHIFT_LIMIT)
        def _():
            def body(j, carry):
                ls_a, ls_b, acc = carry
                kt, vt = key_tile(j)
                p_a = jnp.exp2(_dot_nt(qa, kt) - bound)
                p_b = jnp.exp2(_dot_nt(qb, kt) - bound)
                for i in range(tk // LANES):
                    ls_a = ls_a + p_a[:, i * LANES:(i + 1) * LANES]
                    ls_b = ls_b + p_b[:, i * LANES:(i + 1) * LANES]
                pv_a = _dot(p_a.astype(BF16), vt)
                pv_b = _dot(p_b.astype(BF16), vt)
                return ls_a, ls_b, acc + jnp.where(low, pv_a, pv_b)

            z = jnp.zeros((tq, LANES), F32)
            ls_a, ls_b, acc = lax.fori_loop(0, n // tk, body, (z, z, z), unroll=unroll)
            l_a = jnp.sum(ls_a, axis=1, keepdims=True)
            l_b = jnp.sum(ls_b, axis=1, keepdims=True)
            o_ref[:, cols] = (acc / jnp.where(low, l_a, l_b)).astype(BF16)

        @pl.when(bound > SOFTMAX_SHIFT_LIMIT)
        def _():
            def body(j, carry):
                m_a, l_a, m_b, l_b, acc = carry
                kt, vt = key_tile(j)
                s_a = _dot_nt(qa, kt)
                s_b = _dot_nt(qb, kt)
                mn_a = jnp.maximum(m_a, jnp.max(s_a, axis=-1, keepdims=True))
                mn_b = jnp.maximum(m_b, jnp.max(s_b, axis=-1, keepdims=True))
                p_a = jnp.exp2(s_a - mn_a)
                p_b = jnp.exp2(s_b - mn_b)
                al_a = jnp.exp2(m_a - mn_a)
                al_b = jnp.exp2(m_b - mn_b)
                l_a = al_a * l_a + jnp.sum(p_a, axis=-1, keepdims=True)
                l_b = al_b * l_b + jnp.sum(p_b, axis=-1, keepdims=True)
                pv_a = _dot(p_a.astype(BF16), vt)
                pv_b = _dot(p_b.astype(BF16), vt)
                acc = acc * jnp.where(low, al_a, al_b) + jnp.where(low, pv_a, pv_b)
                return mn_a, l_a, mn_b, l_b, acc

            neg = jnp.full((tq, 1), NEG_INF, F32)
            zer = jnp.zeros((tq, 1), F32)
            _, l_a, _, l_b, acc = lax.fori_loop(0, n // tk, body, (neg, zer, neg, zer, jnp.zeros((tq, LANES), F32)))
            o_ref[:, cols] = (acc / jnp.where(low, l_a, l_b)).astype(BF16)


def _gqa_attn(bound, q, k, v, b, n, tq, tk):
    t = q.shape[0]
    dq = N_HEADS * HEAD_DIM
    nq = n // tq
    pairs = A_KV_HEADS // 2
    qw = dq // pairs
    return pl.pallas_call(
        functools.partial(_gqa_attn_kernel, n=n, tk=tk, unroll=min(n // tk, GQA_UNROLL)),
        grid=(b, pairs, nq),
        in_specs=[
            pl.BlockSpec(memory_space=pltpu.SMEM),
            pl.BlockSpec((tq, qw), lambda bi, p, i: (bi * nq + i, p)),
            pl.BlockSpec((n, LANES), lambda bi, p, i: (bi, p)),
            pl.BlockSpec((n, LANES), lambda bi, p, i: (bi, p)),
        ],
        out_specs=pl.BlockSpec((tq, qw), lambda bi, p, i: (bi * nq + i, p)),
        out_shape=jax.ShapeDtypeStruct((t, dq), BF16),
        compiler_params=_params(3),
        name="gqa_attn",
    )(bound, q, k, v)


def _na_qkv_kernel(x_ref, g_ref, w_ref, bd_ref, gains_ref, q_ref, k_ref, v_ref):
    dh = N_HEADS * HEAD_DIM
    h = _rms(x_ref[...], g_ref[...]).astype(BF16)
    qkv = _dot(h, w_ref[...])
    qk = _head_norm(qkv[:, :2 * dh], bd_ref[...], gains_ref[...])
    q_ref[...] = qk[:, :dh].astype(BF16)
    k_ref[...] = qk[:, dh:].astype(BF16)
    v_ref[...] = qkv[:, 2 * dh:].astype(BF16)


def _na_qkv(x, g, w, bd, gains, tm):
    t = x.shape[0]
    dh = N_HEADS * HEAD_DIM
    full = lambda shape: pl.BlockSpec(shape, lambda i: (0, 0))
    return pl.pallas_call(
        _na_qkv_kernel,
        grid=(t // tm,),
        in_specs=[
            pl.BlockSpec((tm, D_MODEL), lambda i: (i, 0)),
            full((1, D_MODEL)),
            full((D_MODEL, 3 * dh)),
            full((MXU_DIM, MXU_DIM)),
            full((1, 2 * dh)),
        ],
        out_specs=[pl.BlockSpec((tm, dh), lambda i: (i, 0))] * 3,
        out_shape=[jax.ShapeDtypeStruct((t, dh), BF16)] * 3,
        compiler_params=_params(1),
        name="na_qkv",
    )(x, g, w, bd, gains)


def _na_attn_kernel(bound_ref, q_ref, k_ref, v_ref, bias_ref, o_ref, *, nblk, sub):
    ub = pl.program_id(2)
    blk = NA_QB * GRID_W
    win = 3 * blk
    lane = lax.broadcasted_iota(jnp.int32, (blk, LANES), 1)
    low = lane < HEAD_DIM
    bound = bound_ref[0]

    def block(s, softmax):
        u = ub * sub + s
        var = jnp.where(u == 0, 0, jnp.where(u == nblk - 1, 2, 1))
        start = pl.multiple_of(jnp.clip(u - 1, 0, nblk - 3) * blk, blk)
        kw = k_ref[pl.ds(start, win), :]
        vw = v_ref[pl.ds(start, win), :]
        qs = q_ref[s * blk:(s + 1) * blk, :]
        zero = jnp.zeros_like(qs)
        halves = []
        for hh, qm in enumerate((jnp.where(low, qs, zero), jnp.where(low, zero, qs))):
            p, l = softmax(_dot_nt(qm, kw) + bias_ref[var, hh])
            halves.append(_dot(p.astype(BF16), vw) / l)
        o_ref[s * blk:(s + 1) * blk, :] = jnp.where(low, halves[0], halves[1]).astype(BF16)

    def shifted(sc):
        p = jnp.exp2(sc - bound)
        ls = p[:, :LANES]
        for i in range(1, win // LANES):
            ls = ls + p[:, i * LANES:(i + 1) * LANES]
        return p, jnp.sum(ls, axis=1, keepdims=True)

    def running_max(sc):
        p = jnp.exp2(sc - jnp.max(sc, axis=-1, keepdims=True))
        return p, jnp.sum(p, axis=-1, keepdims=True)

    @pl.when(bound <= SOFTMAX_SHIFT_LIMIT)
    def _():
        for s in range(sub):
            block(s, shifted)

    @pl.when(bound > SOFTMAX_SHIFT_LIMIT)
    def _():
        for s in range(sub):
            block(s, running_max)


def _na_attn(bound, q, k, v, bias, b, n, sub):
    t = q.shape[0]
    dh = N_HEADS * HEAD_DIM
    blk = NA_QB * GRID_W
    nblk = n // blk
    assert nblk >= 3 and nblk % sub == 0
    nstep = nblk // sub
    tq = sub * blk
    return pl.pallas_call(
        functools.partial(_na_attn_kernel, nblk=nblk, sub=sub),
        grid=(N_HEADS // 2, b, nstep),
        in_specs=[
            pl.BlockSpec(memory_space=pltpu.SMEM),
            pl.BlockSpec((tq, LANES), lambda p, bi, i: (bi * nstep + i, p)),
            pl.BlockSpec((n, LANES), lambda p, bi, i: (bi, p)),
            pl.BlockSpec((n, LANES), lambda p, bi, i: (bi, p)),
            pl.BlockSpec((3, 2, blk, 3 * blk), lambda p, bi, i: (0, p, 0, 0)),
        ],
        out_specs=pl.BlockSpec((tq, LANES), lambda p, bi, i: (bi * nstep + i, p)),
        out_shape=jax.ShapeDtypeStruct((t, dh), BF16),
        compiler_params=_params(3),
        name="na_attn",
    )(bound, q, k, v, bias)


def _na_bias_table(rpb):
    nkr = 3 * NA_QB
    i = np.arange(NA_QB)[:, None]
    kr = np.arange(nkr)[None, :]
    variants = [(0 * i, NA_ROWS - 1), (i, NA_ROWS - 1 - NA_QB), (NA_QB + 0 * i, NA_ROWS - 1 - 2 * NA_QB)]
    row_valid = np.stack([(kr >= r0) & (kr < r0 + NA_ROWS) for r0, _ in variants])
    dr = np.stack([np.clip(kr - i + off, 0, 2 * NA_ROWS - 2) for _, off in variants])
    row_sel = (dr[..., None] == np.arange(2 * NA_ROWS - 1)).astype(np.float32)
    qc = np.arange(GRID_W)[:, None]
    kc = np.arange(GRID_W)[None, :]
    win0 = np.clip(qc - NA_KW // 2, 0, GRID_W - NA_KW)
    col_valid = (kc >= win0) & (kc < win0 + NA_KW)
    dc = np.clip(kc - qc + NA_KW - 1, 0, 2 * NA_KW - 2)
    col_sel = (dc[..., None] == np.arange(2 * NA_KW - 1)).astype(np.float32)
    hp = lax.Precision.HIGHEST
    tmp = jnp.einsum("vikd,hde->vhike", row_sel, rpb.astype(F32), precision=hp)
    tbl = jnp.einsum("vhike,qce->vhiqkc", tmp, col_sel, precision=hp)
    valid = row_valid[:, :, None, :, None] & col_valid[None, None, :, None, :]
    tbl = jnp.where(valid[:, None], tbl * LOG2_E, NEG_INF)
    return tbl.reshape(3, N_HEADS, NA_QB * GRID_W, nkr * GRID_W)


def _proj_router_kernel(o_ref, w_ref, x_ref, g_ref, rhi_ref, rlo_ref, x1_ref, hf_ref, affc_ref, afft_ref):
    x1 = x_ref[...] + _dot(o_ref[...], w_ref[...])
    x1_ref[...] = x1
    hf = _rms(x1, g_ref[...])
    hi, lo = _split_bf16(hf)
    hf_ref[...] = hi
    logits = _dot(hi, rhi_ref[...]) + (_dot(hi, rlo_ref[...]) + _dot(lo, rhi_ref[...]))
    lane = lax.broadcasted_iota(jnp.int32, logits.shape, 1)
    logits = jnp.where(lane < N_EXPERTS, logits, NEG_INF)
    m = jnp.max(logits, axis=-1, keepdims=True)
    e = jnp.exp(logits - m)
    aff = e / jnp.sum(e, axis=-1, keepdims=True)
    affc_ref[...] = aff[:, :N_EXPERTS]
    afft_ref[...] = aff.T[:N_EXPERTS, :]


def _proj_router(o, w_o, x, g, r_hi, r_lo, tm):
    t = x.shape[0]
    full = lambda shape: pl.BlockSpec(shape, lambda i: (0, 0))
    return pl.pallas_call(
        _proj_router_kernel,
        grid=(t // tm,),
        in_specs=[
            pl.BlockSpec((tm, D_MODEL), lambda i: (i, 0)),
            full((D_MODEL, D_MODEL)),
            pl.BlockSpec((tm, D_MODEL), lambda i: (i, 0)),
            full((1, D_MODEL)),
            full((D_MODEL, LANES)),
            full((D_MODEL, LANES)),
        ],
        out_specs=[
            pl.BlockSpec((tm, D_MODEL), lambda i: (i, 0)),
            pl.BlockSpec((tm, D_MODEL), lambda i: (i, 0)),
            pl.BlockSpec((tm, N_EXPERTS), lambda i: (i, 0)),
            pl.BlockSpec((N_EXPERTS, tm), lambda i: (0, i)),
        ],
        out_shape=[
            jax.ShapeDtypeStruct((t, D_MODEL), F32),
            jax.ShapeDtypeStruct((t, D_MODEL), BF16),
            jax.ShapeDtypeStruct((t, N_EXPERTS), F32),
            jax.ShapeDtypeStruct((N_EXPERTS, t), F32),
        ],
        compiler_params=_params(1),
        name="proj_router",
    )(o, w_o, x, g, r_hi, r_lo)


def _select_kernel(aff_ref, upper_ref, lower_ref, pos_ref, offs_ref, *, cap):
    aff = aff_ref[...]
    ne, nr, _ = aff.shape

    def count(mask):
        return jnp.sum(jnp.sum(jnp.where(mask, 1.0, 0.0), axis=2, keepdims=True), axis=1, keepdims=True)

    def bisect(i, bits):
        cand = bits | jnp.left_shift(jnp.int32(1), 30 - i)
        return jnp.where(count(aff >= pltpu.bitcast(cand, F32)) >= cap, cand, bits)

    thr = pltpu.bitcast(lax.fori_loop(0, 31, bisect, jnp.zeros((ne, 1, 1), jnp.int32)), F32)
    above = aff > thr
    tied = aff == thr
    need = cap - count(above)

    ones = jnp.ones((LANES, LANES), BF16)

    def prefix(mask):
        mb = jnp.where(mask, 1.0, 0.0).astype(BF16).reshape(ne * nr, LANES)
        incl = _dot(mb, upper_ref[...]).reshape(ne, nr, LANES)
        tot = _dot(mb, ones).astype(BF16).reshape(ne, nr, LANES)
        offs = jnp.stack([_dot(lower_ref[...], tot[e]) for e in range(ne)], axis=0)
        return offs + incl - mb.astype(F32).reshape(ne, nr, LANES), offs

    tie_rank, _ = prefix(tied)
    sel = above | (tied & (tie_rank < need))
    pos, offs = prefix(sel)
    pos_ref[...] = jnp.where(sel, pos, -1.0).astype(jnp.int32)
    offs_ref[...] = offs.astype(jnp.int32)


def _select(afft, cap):
    ne, n = afft.shape
    nr = n // LANES
    aff3 = afft.reshape(ne, nr, LANES)
    upper = jnp.asarray(np.triu(np.ones((LANES, LANES), np.float32)), BF16)
    lower = jnp.asarray(np.tril(np.ones((nr, nr), np.float32), -1), BF16)
    blk3 = pl.BlockSpec((ne, nr, LANES), lambda i: (0, 0, 0))
    return pl.pallas_call(
        functools.partial(_select_kernel, cap=cap),
        grid=(1,),
        in_specs=[blk3, pl.BlockSpec((LANES, LANES), lambda i: (0, 0)), pl.BlockSpec((nr, nr), lambda i: (0, 0))],
        out_specs=[blk3, blk3],
        out_shape=[jax.ShapeDtypeStruct((ne, nr, LANES), jnp.int32)] * 2,
        compiler_params=_params(1),
        name="ec_select",
    )(aff3, upper, lower)


def _align_down(x):
    return lax.shift_left(lax.shift_right_logical(x, ROW_ALIGN.bit_length() - 1), ROW_ALIGN.bit_length() - 1)


def _window_info(st_ref, e, t, nt):
    start = st_ref[e * (nt + 1) + t]
    end = st_ref[e * (nt + 1) + t + 1]
    base = _align_down(start)
    return base, end, lax.shift_right_logical(end - base, MOE_WIN.bit_length() - 1) + 1


def _dispatch_kernel(st_ref, hf_ref, pos_ref, xe_ref, stage, extra, carry, sems, xsem, *, nt, cap):
    t = pl.program_id(0)
    ne, tt = pos_ref.shape
    slot = t % 2
    hf = hf_ref[...]
    rows = lax.broadcasted_iota(jnp.int32, (MOE_WIN, tt), 0)
    info = [_window_info(st_ref, e, t, nt) for e in range(ne)]

    @pl.when(t == 0)
    def _():
        carry[...] = jnp.zeros_like(carry)
        extra[...] = jnp.zeros_like(extra)
        pads = [pltpu.make_async_copy(extra.at[pl.ds(e * MOE_WIN, MOE_WIN), :], xe_ref.at[e, pl.ds(cap, MOE_WIN), :], xsem)
                for e in range(ne)]
        for cp in pads:
            cp.start()
        for cp in pads:
            cp.wait()

    def gathered(k):
        parts = []
        for e in range(ne):
            rel = pos_ref[e:e + 1, :] - (info[e][0] + k * MOE_WIN)
            parts.append(jnp.where(rel == rows, 1.0, 0.0).astype(BF16))
        return _dot(jnp.concatenate(parts, axis=0), hf).astype(BF16)

    def window(buf, e, k, sem):
        row = pl.multiple_of(info[e][0] + k * MOE_WIN, ROW_ALIGN)
        return pltpu.make_async_copy(buf.at[pl.ds(e * MOE_WIN, MOE_WIN), :], xe_ref.at[e, pl.ds(row, MOE_WIN), :], sem)

    stage[slot] = gathered(0)
    for e in range(ne):
        base, end, nwin = info[e]
        head = pl.ds(e * MOE_WIN, ROW_ALIGN)
        stage[slot, head, :] = stage[slot, head, :] + carry[e]

        @pl.when(nwin == 1)
        def _():
            tail = pl.multiple_of(e * MOE_WIN + _align_down(end) - base, ROW_ALIGN)
            carry[e] = stage[slot, pl.ds(tail, ROW_ALIGN), :]

        @pl.when(t > 0)
        def _():
            window(stage.at[1 - slot], e, 0, sems.at[e]).wait()

        window(stage.at[slot], e, 0, sems.at[e]).start()

    def more(k, c):
        extra[...] = gathered(k)
        for e in range(ne):
            base, end, nwin = info[e]

            @pl.when(k < nwin)
            def _():
                cp = window(extra, e, k, xsem)
                cp.start()
                cp.wait()

            @pl.when(k == nwin - 1)
            def _():
                tail = pl.multiple_of(e * MOE_WIN + _align_down(end) - base - k * MOE_WIN, ROW_ALIGN)
                carry[e] = extra[pl.ds(tail, ROW_ALIGN), :]
        return c

    lax.fori_loop(1, functools.reduce(jnp.maximum, [i[2] for i in info]), more, 0)

    @pl.when(t == nt - 1)
    def _():
        for e in range(ne):
            window(stage.at[slot], e, 0, sems.at[e]).wait()


def _dispatch(starts, hf, posm, cap, tt):
    ne, n = posm.shape
    nt = n // tt
    grid_spec = pltpu.PrefetchScalarGridSpec(
        num_scalar_prefetch=1,
        grid=(nt,),
        in_specs=[
            pl.BlockSpec((tt, D_MODEL), lambda t, st: (t, 0)),
            pl.BlockSpec((ne, tt), lambda t, st: (0, t)),
        ],
        out_specs=pl.BlockSpec(memory_space=pl.ANY),
        scratch_shapes=[
            pltpu.VMEM((2, ne * MOE_WIN, D_MODEL), BF16),
            pltpu.VMEM((ne * MOE_WIN, D_MODEL), BF16),
            pltpu.VMEM((ne, ROW_ALIGN, D_MODEL), BF16),
            pltpu.SemaphoreType.DMA((ne,)),
            pltpu.SemaphoreType.DMA(()),
        ],
    )
    return pl.pallas_call(
        functools.partial(_dispatch_kernel, nt=nt, cap=cap),
        grid_spec=grid_spec,
        out_shape=jax.ShapeDtypeStruct((ne, cap + MOE_WIN, D_MODEL), BF16),
        compiler_params=_params(1),
        name="ec_dispatch",
    )(starts, hf, posm)


def _combine_kernel(st_ref, y_ref, posc_ref, affc_ref, x_ref, out_ref, ybuf, extra, sems, xsem, *, nt, cap):
    t = pl.program_id(0)
    tt, ne = posc_ref.shape
    slot = t % 2
    lanes = lax.broadcasted_iota(jnp.int32, (tt, MOE_WIN), 1)
    last = cap - MOE_WIN

    def main_row(tile, e):
        return jnp.minimum(_window_info(st_ref, e, tile, nt)[0], last)

    def window(row, e, buf, sem):
        src = y_ref.at[e, pl.ds(pl.multiple_of(row, ROW_ALIGN), MOE_WIN), :]
        return pltpu.make_async_copy(src, buf.at[pl.ds(e * MOE_WIN, MOE_WIN), :], sem)

    @pl.when(t == 0)
    def _():
        extra[...] = jnp.zeros_like(extra)
        for e in range(ne):
            window(main_row(0, e), e, ybuf.at[0], sems.at[0, e]).start()

    @pl.when(t + 1 < nt)
    def _():
        for e in range(ne):
            window(main_row(t + 1, e), e, ybuf.at[1 - slot], sems.at[1 - slot, e]).start()

    row0 = [main_row(t, e) for e in range(ne)]
    for e in range(ne):
        window(row0[e], e, ybuf.at[slot], sems.at[slot, e]).wait()

    def gates(rows, floors):
        his, los = [], []
        for e in range(ne):
            pos = posc_ref[:, e:e + 1]
            hit = (pos - rows[e]) == lanes
            if floors is not None:
                hit = jnp.logical_and(hit, pos >= floors[e])
            hi, lo = _split_bf16(jnp.where(hit, affc_ref[:, e:e + 1], 0.0))
            his.append(hi)
            los.append(lo)
        return jnp.concatenate(his, axis=1), jnp.concatenate(los, axis=1)

    hi, lo = gates(row0, None)
    y = ybuf[slot]
    out_ref[...] = x_ref[...] + (_dot(hi, y) + _dot(lo, y))

    ends = [_window_info(st_ref, e, t, nt)[1] for e in range(ne)]
    nwin = [lax.shift_right_logical(jnp.maximum(ends[e] - row0[e], 1) - 1, MOE_WIN.bit_length() - 1) + 1 for e in range(ne)]

    def more(k, c):
        floors = [row0[e] + k * MOE_WIN for e in range(ne)]
        rows = [jnp.minimum(f, last) for f in floors]
        for e in range(ne):
            @pl.when(k < nwin[e])
            def _():
                cp = window(rows[e], e, extra, xsem)
                cp.start()
                cp.wait()
        hi, lo = gates(rows, floors)
        y = extra[...]
        out_ref[...] += _dot(hi, y) + _dot(lo, y)
        return c

    lax.fori_loop(1, functools.reduce(jnp.maximum, nwin), more, 0)


def _combine(starts, y, posc, affc, x, tt):
    n, ne = posc.shape
    nt = n // tt
    grid_spec = pltpu.PrefetchScalarGridSpec(
        num_scalar_prefetch=1,
        grid=(nt,),
        in_specs=[
            pl.BlockSpec(memory_space=pl.ANY),
            pl.BlockSpec((tt, ne), lambda t, st: (t, 0)),
            pl.BlockSpec((tt, ne), lambda t, st: (t, 0)),
            pl.BlockSpec((tt, D_MODEL), lambda t, st: (t, 0)),
        ],
        out_specs=pl.BlockSpec((tt, D_MODEL), lambda t, st: (t, 0)),
        scratch_shapes=[
            pltpu.VMEM((2, ne * MOE_WIN, D_MODEL), BF16),
            pltpu.VMEM((ne * MOE_WIN, D_MODEL), BF16),
            pltpu.SemaphoreType.DMA((2, ne)),
            pltpu.SemaphoreType.DMA(()),
        ],
    )
    return pl.pallas_call(
        functools.partial(_combine_kernel, nt=nt, cap=y.shape[1]),
        grid_spec=grid_spec,
        out_shape=jax.ShapeDtypeStruct((n, D_MODEL), F32),
        compiler_params=_params(1),
        name="ec_combine",
    )(starts, y, posc, affc, x)


def _ffn_kernel(x_ref, wg_ref, wu_ref, wd_ref, y_ref, *, fc):
    x = x_ref[0]
    acc = jnp.zeros(y_ref.shape[1:], F32)
    for c in range(EXPERT_FF // fc):
        sl = slice(c * fc, (c + 1) * fc)
        a = _dot(x, wg_ref[0, :, sl])
        u = _dot(x, wu_ref[0, :, sl])
        hid = (a * (1.0 / (1.0 + jnp.exp(-a))) * u).astype(BF16)
        acc = acc + _dot(hid, wd_ref[0, sl, :])
    y_ref[0] = acc.astype(BF16)


def _ffn(xe, wg, wu, wd, cap, bm, fc):
    ne = wg.shape[0]
    return pl.pallas_call(
        functools.partial(_ffn_kernel, fc=fc),
        grid=(ne, cap // bm),
        in_specs=[
            pl.BlockSpec((1, bm, D_MODEL), lambda e, j: (e, j, 0)),
            pl.BlockSpec((1, D_MODEL, EXPERT_FF), lambda e, j: (e, 0, 0)),
            pl.BlockSpec((1, D_MODEL, EXPERT_FF), lambda e, j: (e, 0, 0)),
            pl.BlockSpec((1, EXPERT_FF, D_MODEL), lambda e, j: (e, 0, 0)),
        ],
        out_specs=pl.BlockSpec((1, bm, D_MODEL), lambda e, j: (e, j, 0)),
        out_shape=jax.ShapeDtypeStruct((ne, cap, D_MODEL), BF16),
        compiler_params=_params(2),
        name="ec_ffn",
    )(xe, wg, wu, wd)


def _tile(n, pref):
    return pref if n % pref == 0 else n


def _rope_tables(n):
    tok = jnp.arange(n, dtype=jnp.int32)
    pos = jnp.stack([tok // GRID_W, tok % GRID_W], axis=-1).astype(F32)
    inv = ROPE_THETA ** (-jnp.arange(0, ROPE_AXIS_DIM, 2, dtype=F32) / ROPE_AXIS_DIM)
    ang = pos[:, :, None] * inv
    cos, sin = jnp.cos(ang), jnp.sin(ang)
    half = ROPE_AXIS_DIM // 2
    cos_h = jnp.concatenate([cos, cos], axis=-1).reshape(n, HEAD_DIM)
    zeros = jnp.zeros_like(sin)
    s1_h = jnp.concatenate([zeros, sin], axis=-1).reshape(n, HEAD_DIM)
    s2_h = jnp.concatenate([-sin, zeros], axis=-1).reshape(n, HEAD_DIM)
    del half
    rep = lambda a: jnp.concatenate([a, a], axis=-1)
    return rep(cos_h), rep(s1_h), rep(s2_h)


def _block_diag_ones():
    idx = np.arange(MXU_DIM) // HEAD_DIM
    return jnp.asarray((idx[:, None] == idx[None, :]).astype(np.float32), BF16)


_A_GROUP = N_HEADS // A_KV_HEADS
_Q_PERM = np.array([kvp * 2 * _A_GROUP + half * _A_GROUP + g
                    for kvp in range(A_KV_HEADS // 2) for g in range(_A_GROUP) for half in range(2)])


def _perm_head_cols(w):
    d = w.shape[0]
    return w.reshape(d, N_HEADS, HEAD_DIM)[:, _Q_PERM].reshape(d, N_HEADS * HEAD_DIM)


def _moe(x1, hf, affc, afft, wg, wu, wd):
    n = x1.shape[0]
    cap = max(1, CAPACITY_FACTOR * n // N_EXPERTS)
    tt = _tile(n, MOE_TILE)
    bm = _tile(cap, 512)
    pos3, offs3 = _select(afft, cap)
    starts = jnp.concatenate([offs3[:, ::tt // LANES, 0], jnp.full((N_EXPERTS, 1), cap, jnp.int32)], axis=1)
    starts = starts.reshape(-1)
    posm = pos3.reshape(N_EXPERTS, n)
    xe = _dispatch(starts, hf, posm, cap, tt)
    y = _ffn(xe, wg, wu, wd, cap, bm, 512)
    return _combine(starts, y, posm.T, affc, x1, tt)


def _trunk(x, p):
    b, n, _ = x.shape
    t = b * n
    tm = _tile(n, 512)
    x2 = x.reshape(t, D_MODEL)
    for i in range(p["depth"]):
        j = i // 2
        g_mix = p["norm_mix"][i].reshape(1, D_MODEL)
        if i % 2 == 0:
            q, k, v = _gqa_qkv(x2, g_mix, p["a_w_qkv"][j], p["bd"], p["a_gains"][j], *p["rope"][n], n, tm)
            o = _gqa_attn(p["a_bound"][j], q, k, v, b, n, _tile(n, 256), _tile(n, 512))
            w_o = p["a_w_o"][j]
        else:
            q, k, v = _na_qkv(x2, g_mix, p["b_w_qkv"][j], p["bd"], p["b_gains"][j], tm)
            o = _na_attn(p["b_bound"][j], q, k, v, p["b_bias"][j], b, n, 2)
            w_o = p["b_w_o"][j]
        x1, hf, affc, afft = _proj_router(o, w_o, x2, p["norm_ffn"][i].reshape(1, D_MODEL),
                                          p["r_hi"][i], p["r_lo"][i], tm)
        x2 = _moe(x1, hf, affc, afft, p["wg"][i], p["wu"][i], p["wd"][i])
    return x2.reshape(b, n, D_MODEL)


def kernel(x_prompt, x_sample, norm_mix, norm_ffn, a_w_qkv, a_q_norm, a_k_norm, a_w_o, b_w_qkv, b_q_norm, b_k_norm,
           b_rpb, b_w_o, moe_router, moe_w_gate, moe_w_up, moe_w_down):
    depth = norm_mix.shape[0]
    dq = N_HEADS * HEAD_DIM
    n_a, n_b = a_w_qkv.shape[0], b_w_qkv.shape[0]
    a_w = jnp.concatenate([jnp.stack([_perm_head_cols(a_w_qkv[j, :, :dq]) for j in range(n_a)]),
                           a_w_qkv[:, :, dq:]], axis=-1).astype(BF16)
    a_wo = jnp.stack([a_w_o[j].reshape(N_HEADS, HEAD_DIM, D_MODEL)[_Q_PERM].reshape(dq, D_MODEL)
                      for j in range(n_a)]).astype(BF16)
    a_gains = jnp.concatenate([jnp.tile(a_q_norm, (1, N_HEADS)) * Q_GAIN_SCALE,
                               jnp.tile(a_k_norm, (1, A_KV_HEADS))], axis=-1).reshape(n_a, 1, -1)
    b_gains = jnp.concatenate([jnp.tile(b_q_norm, (1, N_HEADS)) * Q_GAIN_SCALE,
                               jnp.tile(b_k_norm, (1, N_HEADS))], axis=-1).reshape(n_b, 1, -1)
    r_pad = jnp.pad(moe_router, ((0, 0), (0, 0), (0, LANES - N_EXPERTS)))
    r_hi = r_pad.astype(BF16)
    r_lo = (r_pad - r_hi.astype(F32)).astype(BF16)
    a_bound = (HEAD_DIM * 1.01) * (jnp.max(jnp.abs(a_gains[:, 0, :dq]), axis=-1, keepdims=True)
                                   * jnp.max(jnp.abs(a_gains[:, 0, dq:]), axis=-1, keepdims=True))
    b_bound = (HEAD_DIM * 1.01) * (jnp.max(jnp.abs(b_gains[:, 0, :dq]), axis=-1, keepdims=True)
                                   * jnp.max(jnp.abs(b_gains[:, 0, dq:]), axis=-1, keepdims=True))
    b_bound = b_bound + LOG2_E * jnp.max(jnp.abs(b_rpb), axis=(1, 2, 3)).reshape(n_b, 1)
    p = {
        "depth": depth, "a_bound": a_bound, "b_bound": b_bound,
        "norm_mix": norm_mix, "norm_ffn": norm_ffn,
        "a_w_qkv": a_w, "a_w_o": a_wo, "a_gains": a_gains,
        "b_w_qkv": b_w_qkv.astype(BF16), "b_w_o": b_w_o.astype(BF16), "b_gains": b_gains,
        "b_bias": jnp.stack([_na_bias_table(b_rpb[j]) for j in range(n_b)]),
        "bd": _block_diag_ones(),
        "r_hi": r_hi, "r_lo": r_lo,
        "wg": moe_w_gate.astype(BF16), "wu": moe_w_up.astype(BF16), "wd": moe_w_down.astype(BF16),
        "rope": {n: _rope_tables(n) for n in {x_prompt.shape[1], x_sample.shape[1]}},
    }
    return _trunk(x_prompt, p), _trunk(x_sample, p)
```

```python
import functools

import jax
import jax.numpy as jnp
import numpy as np
from jax import lax
from jax.experimental import pallas as pl
from jax.experimental.pallas import tpu as pltpu

D_MODEL = 1024
HEAD_DIM = 64
N_HEADS = 16
A_KV_HEADS = 4
GRID_W = 64
ROPE_THETA = 10000.0
ROPE_AXIS_DIM = HEAD_DIM // 2
NA_ROWS = 8
NA_KW = 16
NA_QB = 4
N_EXPERTS = 16
CAPACITY_FACTOR = 2
EXPERT_FF = 2048
RMS_EPS = 1e-6
NEG_INF = -1e30
ATTN_SCALE = HEAD_DIM ** -0.5
LOG2_E = float(np.log2(np.e))
Q_GAIN_SCALE = ATTN_SCALE * LOG2_E
SOFTMAX_SHIFT_LIMIT = 56.0
GQA_UNROLL = 8

LANES = 128
MXU_DIM = 256
VMEM_LIMIT = 48 * 1024 * 1024
ROW_ALIGN = 16
MOE_WIN = 128
MOE_TILE = 512
ROW_CHAIN = 256

BF16 = jnp.bfloat16
F32 = jnp.float32


def _params(n_axes, vmem=VMEM_LIMIT):
    return pltpu.CompilerParams(dimension_semantics=("arbitrary",) * n_axes, vmem_limit_bytes=vmem)


def _dot(a, b):
    return jnp.dot(a, b, preferred_element_type=F32)


def _dot_nt(a, b):
    return lax.dot_general(a, b, (((1,), (1,)), ((), ())), preferred_element_type=F32)


def _rms(x, g):
    ms = jnp.mean(x * x, axis=-1, keepdims=True)
    return x * lax.rsqrt(ms + RMS_EPS) * g


def _split_bf16(x):
    hi = x.astype(BF16)
    lo = (x - hi.astype(F32)).astype(BF16)
    return hi, lo


def _head_norm(y, bd, gains):
    sq = y * y
    hi, lo = _split_bf16(sq)
    w = y.shape[1]
    parts = []
    for c in range(w // MXU_DIM):
        sl = slice(c * MXU_DIM, (c + 1) * MXU_DIM)
        parts.append(_dot(hi[:, sl], bd) + _dot(lo[:, sl], bd))
    ss = jnp.concatenate(parts, axis=1)
    return y * lax.rsqrt(ss * (1.0 / HEAD_DIM) + RMS_EPS) * gains


def _gqa_qkv_kernel(x_ref, g_ref, w_ref, bd_ref, gains_ref, cos_ref, s1_ref, s2_ref, q_ref, k_ref, v_ref):
    dq = N_HEADS * HEAD_DIM
    dkv = A_KV_HEADS * HEAD_DIM
    half = ROPE_AXIS_DIM // 2
    for r in range(x_ref.shape[0] // ROW_CHAIN):
        rows = slice(r * ROW_CHAIN, (r + 1) * ROW_CHAIN)
        h = _rms(x_ref[rows, :], g_ref[...]).astype(BF16)
        qkv = _dot(h, w_ref[...])
        qk = _head_norm(qkv[:, :dq + dkv], bd_ref[...], gains_ref[...])
        cos, s1, s2 = cos_ref[rows, :], s1_ref[rows, :], s2_ref[rows, :]
        outs = []
        for c in range((dq + dkv) // LANES):
            xc = qk[:, c * LANES:(c + 1) * LANES]
            outs.append(xc * cos + pltpu.roll(xc, half, 1) * s1 + pltpu.roll(xc, LANES - half, 1) * s2)
        roped = jnp.concatenate(outs, axis=1)
        q_ref[rows, :] = roped[:, :dq].astype(BF16)
        k_ref[rows, :] = roped[:, dq:].astype(BF16)
        v_ref[rows, :] = qkv[:, dq + dkv:].astype(BF16)


def _gqa_qkv(x, g, w, bd, gains, cos, s1, s2, n, tm):
    t = x.shape[0]
    dq = N_HEADS * HEAD_DIM
    dkv = A_KV_HEADS * HEAD_DIM
    npos = n // tm
    full = lambda shape: pl.BlockSpec(shape, lambda i: (0, 0))
    return pl.pallas_call(
        _gqa_qkv_kernel,
        grid=(t // tm,),
        in_specs=[
            pl.BlockSpec((tm, D_MODEL), lambda i: (i, 0)),
            full((1, D_MODEL)),
            full((D_MODEL, dq + 2 * dkv)),
            full((MXU_DIM, MXU_DIM)),
            full((1, dq + dkv)),
            pl.BlockSpec((tm, LANES), lambda i: (i % npos, 0)),
            pl.BlockSpec((tm, LANES), lambda i: (i % npos, 0)),
            pl.BlockSpec((tm, LANES), lambda i: (i % npos, 0)),
        ],
        out_specs=[
            pl.BlockSpec((tm, dq), lambda i: (i, 0)),
            pl.BlockSpec((tm, dkv), lambda i: (i, 0)),
            pl.BlockSpec((tm, dkv), lambda i: (i, 0)),
        ],
        out_shape=[
            jax.ShapeDtypeStruct((t, dq), BF16),
            jax.ShapeDtypeStruct((t, dkv), BF16),
            jax.ShapeDtypeStruct((t, dkv), BF16),
        ],
        compiler_params=_params(1),
        name="gqa_qkv",
    )(x, g, w, bd, gains, cos, s1, s2)


def _gqa_attn_kernel(bound_ref, q_ref, k_ref, v_ref, o_ref, *, n, tk, unroll):
    tq = q_ref.shape[0]
    lane = lax.broadcasted_iota(jnp.int32, (tq, LANES), 1)
    low = lane < HEAD_DIM
    bound = bound_ref[0]

    def key_tile(j):
        start = pl.multiple_of(j * tk, tk)
        return k_ref[pl.ds(start, tk), :], v_ref[pl.ds(start, tk), :]

    for c in range(q_ref.shape[1] // LANES):
        cols = slice(c * LANES, (c + 1) * LANES)
        qc = q_ref[:, cols]
        zero = jnp.zeros_like(qc)
        qa = jnp.where(low, qc, zero)
        qb = jnp.where(low, zero, qc)

        @pl.when(bound <= SOFTMAX_SHIFT_LIMIT)
        def _():
            def body(j, carry):
                ls_a, ls_b, acc = carry
                kt, vt = key_tile(j)
                p_a = jnp.exp2(_dot_nt(qa, kt) - bound)
                p_b = jnp.exp2(_dot_nt(qb, kt) - bound)
                for i in range(tk // LANES):
                    ls_a = ls_a + p_a[:, i * LANES:(i + 1) * LANES]
                    ls_b = ls_b + p_b[:, i * LANES:(i + 1) * LANES]
                pv_a = _dot(p_a.astype(BF16), vt)
                pv_b = _dot(p_b.astype(BF16), vt)
                return ls_a, ls_b, acc + jnp.where(low, pv_a, pv_b)

            z = jnp.zeros((tq, LANES), F32)
            ls_a, ls_b, acc = lax.fori_loop(0, n // tk, body, (z, z, z), unroll=unroll)
            l_a = jnp.sum(ls_a, axis=1, keepdims=True)
            l_b = jnp.sum(ls_b, axis=1, keepdims=True)
            o_ref[:, cols] = (acc / jnp.where(low, l_a, l_b)).astype(BF16)

        @pl.when(bound > SOFTMAX_SHIFT_LIMIT)
        def _():
            def body(j, carry):
                m_a, l_a, m_b, l_b, acc = carry
                kt, vt = key_tile(j)
                s_a = _dot_nt(qa, kt)
                s_b = _dot_nt(qb, kt)
                mn_a = jnp.maximum(m_a, jnp.max(s_a, axis=-1, keepdims=True))
                mn_b = jnp.maximum(m_b, jnp.max(s_b, axis=-1, keepdims=True))
                p_a = jnp.exp2(s_a - mn_a)
                p_b = jnp.exp2(s_b - mn_b)
                al_a = jnp.exp2(m_a - mn_a)
                al_b = jnp.exp2(m_b - mn_b)
                l_a = al_a * l_a + jnp.sum(p_a, axis=-1, keepdims=True)
                l_b = al_b * l_b + jnp.sum(p_b, axis=-1, keepdims=True)
                pv_a = _dot(p_a.astype(BF16), vt)
                pv_b = _dot(p_b.astype(BF16), vt)
                acc = acc * jnp.where(low, al_a, al_b) + jnp.where(low, pv_a, pv_b)
                return mn_a, l_a, mn_b, l_b, acc

            neg = jnp.full((tq, 1), NEG_INF, F32)
            zer = jnp.zeros((tq, 1), F32)
            _, l_a, _, l_b, acc = lax.fori_loop(0, n // tk, body, (neg, zer, neg, zer, jnp.zeros((tq, LANES), F32)))
            o_ref[:, cols] = (acc / jnp.where(low, l_a, l_b)).astype(BF16)


def _gqa_attn(bound, q, k, v, b, n, tq, tk):
    t = q.shape[0]
    dq = N_HEADS * HEAD_DIM
    nq = n // tq
    pairs = A_KV_HEADS // 2
    qw = dq // pairs
    return pl.pallas_call(
        functools.partial(_gqa_attn_kernel, n=n, tk=tk, unroll=min(n // tk, GQA_UNROLL)),
        grid=(b, pairs, nq),
        in_specs=[
            pl.BlockSpec(memory_space=pltpu.SMEM),
            pl.BlockSpec((tq, qw), lambda bi, p, i: (bi * nq + i, p)),
            pl.BlockSpec((n, LANES), lambda bi, p, i: (bi, p)),
            pl.BlockSpec((n, LANES), lambda bi, p, i: (bi, p)),
        ],
        out_specs=pl.BlockSpec((tq, qw), lambda bi, p, i: (bi * nq + i, p)),
        out_shape=jax.ShapeDtypeStruct((t, dq), BF16),
        compiler_params=_params(3),
        name="gqa_attn",
    )(bound, q, k, v)


def _na_qkv_kernel(x_ref, g_ref, w_ref, bd_ref, gains_ref, q_ref, k_ref, v_ref):
    dh = N_HEADS * HEAD_DIM
    h = _rms(x_ref[...], g_ref[...]).astype(BF16)
    qkv = _dot(h, w_ref[...])
    qk = _head_norm(qkv[:, :2 * dh], bd_ref[...], gains_ref[...])
    q_ref[...] = qk[:, :dh].astype(BF16)
    k_ref[...] = qk[:, dh:].astype(BF16)
    v_ref[...] = qkv[:, 2 * dh:].astype(BF16)


def _na_qkv(x, g, w, bd, gains, tm):
    t = x.shape[0]
    dh = N_HEADS * HEAD_DIM
    full = lambda shape: pl.BlockSpec(shape, lambda i: (0, 0))
    return pl.pallas_call(
        _na_qkv_kernel,
        grid=(t // tm,),
        in_specs=[
            pl.BlockSpec((tm, D_MODEL), lambda i: (i, 0)),
            full((1, D_MODEL)),
            full((D_MODEL, 3 * dh)),
            full((MXU_DIM, MXU_DIM)),
            full((1, 2 * dh)),
        ],
        out_specs=[pl.BlockSpec((tm, dh), lambda i: (i, 0))] * 3,
        out_shape=[jax.ShapeDtypeStruct((t, dh), BF16)] * 3,
        compiler_params=_params(1),
        name="na_qkv",
    )(x, g, w, bd, gains)


def _na_attn_kernel(bound_ref, q_ref, k_ref, v_ref, bias_ref, o_ref, *, nblk, sub):
    ub = pl.program_id(2)
    blk = NA_QB * GRID_W
    win = 3 * blk
    lane = lax.broadcasted_iota(jnp.int32, (blk, LANES), 1)
    low = lane < HEAD_DIM
    bound = bound_ref[0]

    def block(s, softmax):
        u = ub * sub + s
        var = jnp.where(u == 0, 0, jnp.where(u == nblk - 1, 2, 1))
        start = pl.multiple_of(jnp.clip(u - 1, 0, nblk - 3) * blk, blk)
        kw = k_ref[pl.ds(start, win), :]
        vw = v_ref[pl.ds(start, win), :]
        qs = q_ref[s * blk:(s + 1) * blk, :]
        zero = jnp.zeros_like(qs)
        halves = []
        for hh, qm in enumerate((jnp.where(low, qs, zero), jnp.where(low, zero, qs))):
            p, l = softmax(_dot_nt(qm, kw) + bias_ref[var, hh])
            halves.append(_dot(p.astype(BF16), vw) / l)
        o_ref[s * blk:(s + 1) * blk, :] = jnp.where(low, halves[0], halves[1]).astype(BF16)

    def shifted(sc):
        p = jnp.exp2(sc - bound)
        ls = p[:, :LANES]
        for i in range(1, win // LANES):
            ls = ls + p[:, i * LANES:(i + 1) * LANES]
        return p, jnp.sum(ls, axis=1, keepdims=True)

    def running_max(sc):
        p = jnp.exp2(sc - jnp.max(sc, axis=-1, keepdims=True))
        return p, jnp.sum(p, axis=-1, keepdims=True)

    @pl.when(bound <= SOFTMAX_SHIFT_LIMIT)
    def _():
        for s in range(sub):
            block(s, shifted)

    @pl.when(bound > SOFTMAX_SHIFT_LIMIT)
    def _():
        for s in range(sub):
            block(s, running_max)


def _na_attn(bound, q, k, v, bias, b, n, sub):
    t = q.shape[0]
    dh = N_HEADS * HEAD_DIM
    blk = NA_QB * GRID_W
    nblk = n // blk
    assert nblk >= 3 and nblk % sub == 0
    nstep = nblk // sub
    tq = sub * blk
    return pl.pallas_call(
        functools.partial(_na_attn_kernel, nblk=nblk, sub=sub),
        grid=(N_HEADS // 2, b, nstep),
        in_specs=[
            pl.BlockSpec(memory_space=pltpu.SMEM),
            pl.BlockSpec((tq, LANES), lambda p, bi, i: (bi * nstep + i, p)),
            pl.BlockSpec((n, LANES), lambda p, bi, i: (bi, p)),
            pl.BlockSpec((n, LANES), lambda p, bi, i: (bi, p)),
            pl.BlockSpec((3, 2, blk, 3 * blk), lambda p, bi, i: (0, p, 0, 0)),
        ],
        out_specs=pl.BlockSpec((tq, LANES), lambda p, bi, i: (bi * nstep + i, p)),
        out_shape=jax.ShapeDtypeStruct((t, dh), BF16),
        compiler_params=_params(3),
        name="na_attn",
    )(bound, q, k, v, bias)


def _na_bias_table(rpb):
    nkr = 3 * NA_QB
    i = np.arange(NA_QB)[:, None]
    kr = np.arange(nkr)[None, :]
    variants = [(0 * i, NA_ROWS - 1), (i, NA_ROWS - 1 - NA_QB), (NA_QB + 0 * i, NA_ROWS - 1 - 2 * NA_QB)]
    row_valid = np.stack([(kr >= r0) & (kr < r0 + NA_ROWS) for r0, _ in variants])
    dr = np.stack([np.clip(kr - i + off, 0, 2 * NA_ROWS - 2) for _, off in variants])
    row_sel = (dr[..., None] == np.arange(2 * NA_ROWS - 1)).astype(np.float32)
    qc = np.arange(GRID_W)[:, None]
    kc = np.arange(GRID_W)[None, :]
    win0 = np.clip(qc - NA_KW // 2, 0, GRID_W - NA_KW)
    col_valid = (kc >= win0) & (kc < win0 + NA_KW)
    dc = np.clip(kc - qc + NA_KW - 1, 0, 2 * NA_KW - 2)
    col_sel = (dc[..., None] == np.arange(2 * NA_KW - 1)).astype(np.float32)
    hp = lax.Precision.HIGHEST
    tmp = jnp.einsum("vikd,hde->vhike", row_sel, rpb.astype(F32), precision=hp)
    tbl = jnp.einsum("vhike,qce->vhiqkc", tmp, col_sel, precision=hp)
    valid = row_valid[:, :, None, :, None] & col_valid[None, None, :, None, :]
    tbl = jnp.where(valid[:, None], tbl * LOG2_E, NEG_INF)
    return tbl.reshape(3, N_HEADS, NA_QB * GRID_W, nkr * GRID_W)


def _proj_router_kernel(o_ref, w_ref, x_ref, g_ref, rhi_ref, rlo_ref, x1_ref, hf_ref, affc_ref, afft_ref):
    tm = o_ref.shape[0]
    for r in range(tm // ROW_CHAIN):
        rows = slice(r * ROW_CHAIN, (r + 1) * ROW_CHAIN)
        x1 = x_ref[rows, :] + _dot(o_ref[rows, :], w_ref[...])
        x1_ref[rows, :] = x1
        hf = _rms(x1, g_ref[...])
        hi, lo = _split_bf16(hf)
        hf_ref[rows, :] = hi
        logits = _dot(hi, rhi_ref[...]) + (_dot(hi, rlo_ref[...]) + _dot(lo, rhi_ref[...]))
        lane = lax.broadcasted_iota(jnp.int32, logits.shape, 1)
        logits = jnp.where(lane < N_EXPERTS, logits, NEG_INF)
        m = jnp.max(logits, axis=-1, keepdims=True)
        e = jnp.exp(logits - m)
        aff = e / jnp.sum(e, axis=-1, keepdims=True)
        affc_ref[rows, :] = aff[:, :N_EXPERTS]
        afft_ref[:, rows] = aff.T[:N_EXPERTS, :]


def _proj_router(o, w_o, x, g, r_hi, r_lo, tm):
    t = x.shape[0]
    full = lambda shape: pl.BlockSpec(shape, lambda i: (0, 0))
    return pl.pallas_call(
        _proj_router_kernel,
        grid=(t // tm,),
        in_specs=[
            pl.BlockSpec((tm, D_MODEL), lambda i: (i, 0)),
            full((D_MODEL, D_MODEL)),
            pl.BlockSpec((tm, D_MODEL), lambda i: (i, 0)),
            full((1, D_MODEL)),
            full((D_MODEL, LANES)),
            full((D_MODEL, LANES)),
        ],
        out_specs=[
            pl.BlockSpec((tm, D_MODEL), lambda i: (i, 0)),
            pl.BlockSpec((tm, D_MODEL), lambda i: (i, 0)),
            pl.BlockSpec((tm, N_EXPERTS), lambda i: (i, 0)),
            pl.BlockSpec((N_EXPERTS, tm), lambda i: (0, i)),
        ],
        out_shape=[
            jax.ShapeDtypeStruct((t, D_MODEL), F32),
            jax.ShapeDtypeStruct((t, D_MODEL), BF16),
            jax.ShapeDtypeStruct((t, N_EXPERTS), F32),
            jax.ShapeDtypeStruct((N_EXPERTS, t), F32),
        ],
        compiler_params=_params(1),
        name="proj_router",
    )(o, w_o, x, g, r_hi, r_lo)


def _select_kernel(aff_ref, upper_ref, lower_ref, pos_ref, offs_ref, *, cap):
    aff = aff_ref[...]
    ne, nr, _ = aff.shape

    def count(mask):
        return jnp.sum(jnp.sum(jnp.where(mask, 1.0, 0.0), axis=2, keepdims=True), axis=1, keepdims=True)

    def bisect(i, bits):
        cand = bits | jnp.left_shift(jnp.int32(1), 30 - i)
        return jnp.where(count(aff >= pltpu.bitcast(cand, F32)) >= cap, cand, bits)

    thr = pltpu.bitcast(lax.fori_loop(0, 31, bisect, jnp.zeros((ne, 1, 1), jnp.int32)), F32)
    above = aff > thr
    tied = aff == thr
    need = cap - count(above)

    ones = jnp.ones((LANES, LANES), BF16)

    def prefix(mask):
        mb = jnp.where(mask, 1.0, 0.0).astype(BF16).reshape(ne * nr, LANES)
        incl = _dot(mb, upper_ref[...]).reshape(ne, nr, LANES)
        tot = _dot(mb, ones).astype(BF16).reshape(ne, nr, LANES)
        offs = jnp.stack([_dot(lower_ref[...], tot[e]) for e in range(ne)], axis=0)
        return offs + incl - mb.astype(F32).reshape(ne, nr, LANES), offs

    tie_rank, _ = prefix(tied)
    sel = above | (tied & (tie_rank < need))
    pos, offs = prefix(sel)
    pos_ref[...] = jnp.where(sel, pos, -1.0).astype(jnp.int32)
    offs_ref[...] = offs.astype(jnp.int32)


def _select(afft, cap):
    ne, n = afft.shape
    nr = n // LANES
    aff3 = afft.reshape(ne, nr, LANES)
    upper = jnp.asarray(np.triu(np.ones((LANES, LANES), np.float32)), BF16)
    lower = jnp.asarray(np.tril(np.ones((nr, nr), np.float32), -1), BF16)
    blk3 = pl.BlockSpec((ne, nr, LANES), lambda i: (0, 0, 0))
    return pl.pallas_call(
        functools.partial(_select_kernel, cap=cap),
        grid=(1,),
        in_specs=[blk3, pl.BlockSpec((LANES, LANES), lambda i: (0, 0)), pl.BlockSpec((nr, nr), lambda i: (0, 0))],
        out_specs=[blk3, blk3],
        out_shape=[jax.ShapeDtypeStruct((ne, nr, LANES), jnp.int32)] * 2,
        compiler_params=_params(1),
        name="ec_select",
    )(aff3, upper, lower)


def _align_down(x):
    return lax.shift_left(lax.shift_right_logical(x, ROW_ALIGN.bit_length() - 1), ROW_ALIGN.bit_length() - 1)


def _window_info(st_ref, e, t, nt):
    start = st_ref[e * (nt + 1) + t]
    end = st_ref[e * (nt + 1) + t + 1]
    base = _align_down(start)
    return base, end, lax.shift_right_logical(end - base, MOE_WIN.bit_length() - 1) + 1


def _dispatch_kernel(st_ref, hf_ref, pos_ref, xe_ref, stage, extra, carry, sems, xsem, *, nt, cap):
    t = pl.program_id(0)
    ne, tt = pos_ref.shape
    slot = t % 2
    hf = hf_ref[...]
    rows = lax.broadcasted_iota(jnp.int32, (MOE_WIN, tt), 0)
    info = [_window_info(st_ref, e, t, nt) for e in range(ne)]

    @pl.when(t == 0)
    def _():
        carry[...] = jnp.zeros_like(carry)
        extra[...] = jnp.zeros_like(extra)
        pads = [pltpu.make_async_copy(extra.at[pl.ds(e * MOE_WIN, MOE_WIN), :], xe_ref.at[e, pl.ds(cap, MOE_WIN), :], xsem)
                for e in range(ne)]
        for cp in pads:
            cp.start()
        for cp in pads:
            cp.wait()

    def gathered(k):
        parts = []
        for e in range(ne):
            rel = pos_ref[e:e + 1, :] - (info[e][0] + k * MOE_WIN)
            parts.append(jnp.where(rel == rows, 1.0, 0.0).astype(BF16))
        return _dot(jnp.concatenate(parts, axis=0), hf).astype(BF16)

    def window(buf, e, k, sem):
        row = pl.multiple_of(info[e][0] + k * MOE_WIN, ROW_ALIGN)
        return pltpu.make_async_copy(buf.at[pl.ds(e * MOE_WIN, MOE_WIN), :], xe_ref.at[e, pl.ds(row, MOE_WIN), :], sem)

    stage[slot] = gathered(0)
    for e in range(ne):
        base, end, nwin = info[e]
        head = pl.ds(e * MOE_WIN, ROW_ALIGN)
        stage[slot, head, :] = stage[slot, head, :] + carry[e]

        @pl.when(nwin == 1)
        def _():
            tail = pl.multiple_of(e * MOE_WIN + _align_down(end) - base, ROW_ALIGN)
            carry[e] = stage[slot, pl.ds(tail, ROW_ALIGN), :]

        @pl.when(t > 0)
        def _():
            window(stage.at[1 - slot], e, 0, sems.at[e]).wait()

        window(stage.at[slot], e, 0, sems.at[e]).start()

    def more(k, c):
        extra[...] = gathered(k)
        for e in range(ne):
            base, end, nwin = info[e]

            @pl.when(k < nwin)
            def _():
                cp = window(extra, e, k, xsem)
                cp.start()
                cp.wait()

            @pl.when(k == nwin - 1)
            def _():
                tail = pl.multiple_of(e * MOE_WIN + _align_down(end) - base - k * MOE_WIN, ROW_ALIGN)
                carry[e] = extra[pl.ds(tail, ROW_ALIGN), :]
        return c

    lax.fori_loop(1, functools.reduce(jnp.maximum, [i[2] for i in info]), more, 0)

    @pl.when(t == nt - 1)
    def _():
        for e in range(ne):
            window(stage.at[slot], e, 0, sems.at[e]).wait()


def _dispatch(starts, hf, posm, cap, tt):
    ne, n = posm.shape
    nt = n // tt
    grid_spec = pltpu.PrefetchScalarGridSpec(
        num_scalar_prefetch=1,
        grid=(nt,),
        in_specs=[
            pl.BlockSpec((tt, D_MODEL), lambda t, st: (t, 0)),
            pl.BlockSpec((ne, tt), lambda t, st: (0, t)),
        ],
        out_specs=pl.BlockSpec(memory_space=pl.ANY),
        scratch_shapes=[
            pltpu.VMEM((2, ne * MOE_WIN, D_MODEL), BF16),
            pltpu.VMEM((ne * MOE_WIN, D_MODEL), BF16),
            pltpu.VMEM((ne, ROW_ALIGN, D_MODEL), BF16),
            pltpu.SemaphoreType.DMA((ne,)),
            pltpu.SemaphoreType.DMA(()),
        ],
    )
    return pl.pallas_call(
        functools.partial(_dispatch_kernel, nt=nt, cap=cap),
        grid_spec=grid_spec,
        out_shape=jax.ShapeDtypeStruct((ne, cap + MOE_WIN, D_MODEL), BF16),
        compiler_params=_params(1),
        name="ec_dispatch",
    )(starts, hf, posm)


def _combine_kernel(st_ref, y_ref, posc_ref, affc_ref, x_ref, out_ref, ybuf, extra, sems, xsem, *, nt, cap):
    t = pl.program_id(0)
    tt, ne = posc_ref.shape
    slot = t % 2
    last = cap - MOE_WIN

    def main_row(tile, e):
        return jnp.minimum(_window_info(st_ref, e, tile, nt)[0], last)

    def window(row, e, buf, sem):
        src = y_ref.at[e, pl.ds(pl.multiple_of(row, ROW_ALIGN), MOE_WIN), :]
        return pltpu.make_async_copy(src, buf.at[pl.ds(e * MOE_WIN, MOE_WIN), :], sem)

    @pl.when(t == 0)
    def _():
        extra[...] = jnp.zeros_like(extra)
        for e in range(ne):
            window(main_row(0, e), e, ybuf.at[0], sems.at[0, e]).start()

    @pl.when(t + 1 < nt)
    def _():
        for e in range(ne):
            window(main_row(t + 1, e), e, ybuf.at[1 - slot], sems.at[1 - slot, e]).start()

    row0 = [main_row(t, e) for e in range(ne)]
    for e in range(ne):
        window(row0[e], e, ybuf.at[slot], sems.at[slot, e]).wait()

    def gates(rows, floors, toks=slice(None)):
        his, los = [], []
        for e in range(ne):
            pos = posc_ref[toks, e:e + 1]
            hit = (pos - rows[e]) == lax.broadcasted_iota(jnp.int32, (pos.shape[0], MOE_WIN), 1)
            if floors is not None:
                hit = jnp.logical_and(hit, pos >= floors[e])
            hi, lo = _split_bf16(jnp.where(hit, affc_ref[toks, e:e + 1], 0.0))
            his.append(hi)
            los.append(lo)
        return jnp.concatenate(his, axis=1), jnp.concatenate(los, axis=1)

    y = ybuf[slot]
    for r in range(tt // ROW_CHAIN):
        rows = slice(r * ROW_CHAIN, (r + 1) * ROW_CHAIN)
        hi, lo = gates(row0, None, rows)
        out_ref[rows, :] = x_ref[rows, :] + (_dot(hi, y) + _dot(lo, y))

    ends = [_window_info(st_ref, e, t, nt)[1] for e in range(ne)]
    nwin = [lax.shift_right_logical(jnp.maximum(ends[e] - row0[e], 1) - 1, MOE_WIN.bit_length() - 1) + 1 for e in range(ne)]

    def more(k, c):
        floors = [row0[e] + k * MOE_WIN for e in range(ne)]
        rows = [jnp.minimum(f, last) for f in floors]
        for e in range(ne):
            @pl.when(k < nwin[e])
            def _():
                cp = window(rows[e], e, extra, xsem)
                cp.start()
                cp.wait()
        hi, lo = gates(rows, floors)
        y = extra[...]
        out_ref[...] += _dot(hi, y) + _dot(lo, y)
        return c

    lax.fori_loop(1, functools.reduce(jnp.maximum, nwin), more, 0)


def _combine(starts, y, posc, affc, x, tt):
    n, ne = posc.shape
    nt = n // tt
    grid_spec = pltpu.PrefetchScalarGridSpec(
        num_scalar_prefetch=1,
        grid=(nt,),
        in_specs=[
            pl.BlockSpec(memory_space=pl.ANY),
            pl.BlockSpec((tt, ne), lambda t, st: (t, 0)),
            pl.BlockSpec((tt, ne), lambda t, st: (t, 0)),
            pl.BlockSpec((tt, D_MODEL), lambda t, st: (t, 0)),
        ],
        out_specs=pl.BlockSpec((tt, D_MODEL), lambda t, st: (t, 0)),
        scratch_shapes=[
            pltpu.VMEM((2, ne * MOE_WIN, D_MODEL), BF16),
            pltpu.VMEM((ne * MOE_WIN, D_MODEL), BF16),
            pltpu.SemaphoreType.DMA((2, ne)),
            pltpu.SemaphoreType.DMA(()),
        ],
    )
    return pl.pallas_call(
        functools.partial(_combine_kernel, nt=nt, cap=y.shape[1]),
        grid_spec=grid_spec,
        out_shape=jax.ShapeDtypeStruct((n, D_MODEL), F32),
        compiler_params=_params(1),
        name="ec_combine",
    )(starts, y, posc, affc, x)


def _ffn_kernel(x_ref, wg_ref, wu_ref, wd_ref, y_ref, *, fc):
    x = x_ref[0]
    acc = jnp.zeros(y_ref.shape[1:], F32)
    for c in range(EXPERT_FF // fc):
        sl = slice(c * fc, (c + 1) * fc)
        a = _dot(x, wg_ref[0, :, sl])
        u = _dot(x, wu_ref[0, :, sl])
        hid = (a * (1.0 / (1.0 + jnp.exp(-a))) * u).astype(BF16)
        acc = acc + _dot(hid, wd_ref[0, sl, :])
    y_ref[0] = acc.astype(BF16)


def _ffn(xe, wg, wu, wd, cap, bm, fc):
    ne = wg.shape[0]
    return pl.pallas_call(
        functools.partial(_ffn_kernel, fc=fc),
        grid=(ne, cap // bm),
        in_specs=[
            pl.BlockSpec((1, bm, D_MODEL), lambda e, j: (e, j, 0)),
            pl.BlockSpec((1, D_MODEL, EXPERT_FF), lambda e, j: (e, 0, 0)),
            pl.BlockSpec((1, D_MODEL, EXPERT_FF), lambda e, j: (e, 0, 0)),
            pl.BlockSpec((1, EXPERT_FF, D_MODEL), lambda e, j: (e, 0, 0)),
        ],
        out_specs=pl.BlockSpec((1, bm, D_MODEL), lambda e, j: (e, j, 0)),
        out_shape=jax.ShapeDtypeStruct((ne, cap, D_MODEL), BF16),
        compiler_params=_params(2),
        name="ec_ffn",
    )(xe, wg, wu, wd)


def _tile(n, pref):
    return pref if n % pref == 0 else n


def _rope_tables(n):
    tok = jnp.arange(n, dtype=jnp.int32)
    pos = jnp.stack([tok // GRID_W, tok % GRID_W], axis=-1).astype(F32)
    inv = ROPE_THETA ** (-jnp.arange(0, ROPE_AXIS_DIM, 2, dtype=F32) / ROPE_AXIS_DIM)
    ang = pos[:, :, None] * inv
    cos, sin = jnp.cos(ang), jnp.sin(ang)
    half = ROPE_AXIS_DIM // 2
    cos_h = jnp.concatenate([cos, cos], axis=-1).reshape(n, HEAD_DIM)
    zeros = jnp.zeros_like(sin)
    s1_h = jnp.concatenate([zeros, sin], axis=-1).reshape(n, HEAD_DIM)
    s2_h = jnp.concatenate([-sin, zeros], axis=-1).reshape(n, HEAD_DIM)
    del half
    rep = lambda a: jnp.concatenate([a, a], axis=-1)
    return rep(cos_h), rep(s1_h), rep(s2_h)


def _block_diag_ones():
    idx = np.arange(MXU_DIM) // HEAD_DIM
    return jnp.asarray((idx[:, None] == idx[None, :]).astype(np.float32), BF16)


_A_GROUP = N_HEADS // A_KV_HEADS
_Q_PERM = np.array([kvp * 2 * _A_GROUP + half * _A_GROUP + g
                    for kvp in range(A_KV_HEADS // 2) for g in range(_A_GROUP) for half in range(2)])


def _perm_head_cols(w):
    d = w.shape[0]
    return w.reshape(d, N_HEADS, HEAD_DIM)[:, _Q_PERM].reshape(d, N_HEADS * HEAD_DIM)


def _moe(x1, hf, affc, afft, wg, wu, wd):
    n = x1.shape[0]
    cap = max(1, CAPACITY_FACTOR * n // N_EXPERTS)
    tt = _tile(n, MOE_TILE)
    bm = _tile(cap, 512)
    pos3, offs3 = _select(afft, cap)
    starts = jnp.concatenate([offs3[:, ::tt // LANES, 0], jnp.full((N_EXPERTS, 1), cap, jnp.int32)], axis=1)
    starts = starts.reshape(-1)
    posm = pos3.reshape(N_EXPERTS, n)
    xe = _dispatch(starts, hf, posm, cap, tt)
    y = _ffn(xe, wg, wu, wd, cap, bm, 512)
    return _combine(starts, y, posm.T, affc, x1, tt)


def _trunk(x, p):
    b, n, _ = x.shape
    t = b * n
    tm = _tile(n, 512)
    x2 = x.reshape(t, D_MODEL)
    for i in range(p["depth"]):
        j = i // 2
        g_mix = p["norm_mix"][i].reshape(1, D_MODEL)
        if i % 2 == 0:
            q, k, v = _gqa_qkv(x2, g_mix, p["a_w_qkv"][j], p["bd"], p["a_gains"][j], *p["rope"][n], n, tm)
            o = _gqa_attn(p["a_bound"][j], q, k, v, b, n, _tile(n, 256), _tile(n, 512))
            w_o = p["a_w_o"][j]
        else:
            q, k, v = _na_qkv(x2, g_mix, p["b_w_qkv"][j], p["bd"], p["b_gains"][j], tm)
            o = _na_attn(p["b_bound"][j], q, k, v, p["b_bias"][j], b, n, 4)
            w_o = p["b_w_o"][j]
        x1, hf, affc, afft = _proj_router(o, w_o, x2, p["norm_ffn"][i].reshape(1, D_MODEL),
                                          p["r_hi"][i], p["r_lo"][i], tm)
        x2 = _moe(x1, hf, affc, afft, p["wg"][i], p["wu"][i], p["wd"][i])
    return x2.reshape(b, n, D_MODEL)


def kernel(x_prompt, x_sample, norm_mix, norm_ffn, a_w_qkv, a_q_norm, a_k_norm, a_w_o, b_w_qkv, b_q_norm, b_k_norm,
           b_rpb, b_w_o, moe_router, moe_w_gate, moe_w_up, moe_w_down):
    depth = norm_mix.shape[0]
    dq = N_HEADS * HEAD_DIM
    n_a, n_b = a_w_qkv.shape[0], b_w_qkv.shape[0]
    a_w = jnp.concatenate([jnp.stack([_perm_head_cols(a_w_qkv[j, :, :dq]) for j in range(n_a)]),
                           a_w_qkv[:, :, dq:]], axis=-1).astype(BF16)
    a_wo = jnp.stack([a_w_o[j].reshape(N_HEADS, HEAD_DIM, D_MODEL)[_Q_PERM].reshape(dq, D_MODEL)
                      for j in range(n_a)]).astype(BF16)
    a_gains = jnp.concatenate([jnp.tile(a_q_norm, (1, N_HEADS)) * Q_GAIN_SCALE,
                               jnp.tile(a_k_norm, (1, A_KV_HEADS))], axis=-1).reshape(n_a, 1, -1)
    b_gains = jnp.concatenate([jnp.tile(b_q_norm, (1, N_HEADS)) * Q_GAIN_SCALE,
                               jnp.tile(b_k_norm, (1, N_HEADS))], axis=-1).reshape(n_b, 1, -1)
    r_pad = jnp.pad(moe_router, ((0, 0), (0, 0), (0, LANES - N_EXPERTS)))
    r_hi = r_pad.astype(BF16)
    r_lo = (r_pad - r_hi.astype(F32)).astype(BF16)
    a_bound = (HEAD_DIM * 1.01) * (jnp.max(jnp.abs(a_gains[:, 0, :dq]), axis=-1, keepdims=True)
                                   * jnp.max(jnp.abs(a_gains[:, 0, dq:]), axis=-1, keepdims=True))
    b_bound = (HEAD_DIM * 1.01) * (jnp.max(jnp.abs(b_gains[:, 0, :dq]), axis=-1, keepdims=True)
                                   * jnp.max(jnp.abs(b_gains[:, 0, dq:]), axis=-1, keepdims=True))
    b_bound = b_bound + LOG2_E * jnp.max(jnp.abs(b_rpb), axis=(1, 2, 3)).reshape(n_b, 1)
    p = {
        "depth": depth, "a_bound": a_bound, "b_bound": b_bound,
        "norm_mix": norm_mix, "norm_ffn": norm_ffn,
        "a_w_qkv": a_w, "a_w_o": a_wo, "a_gains": a_gains,
        "b_w_qkv": b_w_qkv.astype(BF16), "b_w_o": b_w_o.astype(BF16), "b_gains": b_gains,
        "b_bias": jnp.stack([_na_bias_table(b_rpb[j]) for j in range(n_b)]),
        "bd": _block_diag_ones(),
        "r_hi": r_hi, "r_lo": r_lo,
        "wg": moe_w_gate.astype(BF16), "wu": moe_w_up.astype(BF16), "wd": moe_w_down.astype(BF16),
        "rope": {n: _rope_tables(n) for n in {x_prompt.shape[1], x_sample.shape[1]}},
    }
    return _trunk(x_prompt, p), _trunk(x_sample, p)
```

```python
import functools

import jax
import jax.numpy as jnp
import numpy as np
from jax import lax
from jax.experimental import pallas as pl
from jax.experimental.pallas import tpu as pltpu

D_MODEL = 1024
HEAD_DIM = 64
N_HEADS = 16
A_KV_HEADS = 4
GRID_W = 64
ROPE_THETA = 10000.0
ROPE_AXIS_DIM = HEAD_DIM // 2
NA_ROWS = 8
NA_KW = 16
NA_QB = 4
N_EXPERTS = 16
CAPACITY_FACTOR = 2
EXPERT_FF = 2048
RMS_EPS = 1e-6
NEG_INF = -1e30
ATTN_SCALE = HEAD_DIM ** -0.5
LOG2_E = float(np.log2(np.e))
Q_GAIN_SCALE = ATTN_SCALE * LOG2_E
SOFTMAX_SHIFT_LIMIT = 56.0
GQA_UNROLL = 8

LANES = 128
MXU_DIM = 256
VMEM_LIMIT = 48 * 1024 * 1024
ROW_ALIGN = 16
MOE_WIN = 128
MOE_TILE = 512
GATE_LANES = 2 * LANES
ROW_CHAIN = 256

BF16 = jnp.bfloat16
F32 = jnp.float32


def _params(n_axes, vmem=VMEM_LIMIT):
    return pltpu.CompilerParams(dimension_semantics=("arbitrary",) * n_axes, vmem_limit_bytes=vmem)


def _dot(a, b):
    return jnp.dot(a, b, preferred_element_type=F32)


def _dot_nt(a, b):
    return lax.dot_general(a, b, (((1,), (1,)), ((), ())), preferred_element_type=F32)


def _rms(x, g):
    ms = jnp.mean(x * x, axis=-1, keepdims=True)
    return x * lax.rsqrt(ms + RMS_EPS) * g


def _split_bf16(x):
    hi = x.astype(BF16)
    lo = (x - hi.astype(F32)).astype(BF16)
    return hi, lo


def _head_norm(y, bd, gains):
    sq = y * y
    hi, lo = _split_bf16(sq)
    w = y.shape[1]
    parts = []
    for c in range(w // MXU_DIM):
        sl = slice(c * MXU_DIM, (c + 1) * MXU_DIM)
        parts.append(_dot(hi[:, sl], bd) + _dot(lo[:, sl], bd))
    ss = jnp.concatenate(parts, axis=1)
    return y * lax.rsqrt(ss * (1.0 / HEAD_DIM) + RMS_EPS) * gains


def _gqa_qkv_kernel(x_ref, g_ref, w_ref, bd_ref, gains_ref, cos_ref, s1_ref, s2_ref, q_ref, k_ref, v_ref):
    dq = N_HEADS * HEAD_DIM
    dkv = A_KV_HEADS * HEAD_DIM
    half = ROPE_AXIS_DIM // 2
    for r in range(x_ref.shape[0] // ROW_CHAIN):
        rows = slice(r * ROW_CHAIN, (r + 1) * ROW_CHAIN)
        h = _rms(x_ref[rows, :], g_ref[...]).astype(BF16)
        qkv = _dot(h, w_ref[...])
        qk = _head_norm(qkv[:, :dq + dkv], bd_ref[...], gains_ref[...])
        cos, s1, s2 = cos_ref[rows, :], s1_ref[rows, :], s2_ref[rows, :]
        outs = []
        for c in range((dq + dkv) // LANES):
            xc = qk[:, c * LANES:(c + 1) * LANES]
            outs.append(xc * cos + pltpu.roll(xc, half, 1) * s1 + pltpu.roll(xc, LANES - half, 1) * s2)
        roped = jnp.concatenate(outs, axis=1)
        q_ref[rows, :] = roped[:, :dq].astype(BF16)
        k_ref[rows, :] = roped[:, dq:].astype(BF16)
        v_ref[rows, :] = qkv[:, dq + dkv:].astype(BF16)


def _gqa_qkv(x, g, w, bd, gains, cos, s1, s2, n, tm):
    t = x.shape[0]
    dq = N_HEADS * HEAD_DIM
    dkv = A_KV_HEADS * HEAD_DIM
    npos = n // tm
    full = lambda shape: pl.BlockSpec(shape, lambda i: (0, 0))
    return pl.pallas_call(
        _gqa_qkv_kernel,
        grid=(t // tm,),
        in_specs=[
            pl.BlockSpec((tm, D_MODEL), lambda i: (i, 0)),
            full((1, D_MODEL)),
            full((D_MODEL, dq + 2 * dkv)),
            full((MXU_DIM, MXU_DIM)),
            full((1, dq + dkv)),
            pl.BlockSpec((tm, LANES), lambda i: (i % npos, 0)),
            pl.BlockSpec((tm, LANES), lambda i: (i % npos, 0)),
            pl.BlockSpec((tm, LANES), lambda i: (i % npos, 0)),
        ],
        out_specs=[
            pl.BlockSpec((tm, dq), lambda i: (i, 0)),
            pl.BlockSpec((tm, dkv), lambda i: (i, 0)),
            pl.BlockSpec((tm, dkv), lambda i: (i, 0)),
        ],
        out_shape=[
            jax.ShapeDtypeStruct((t, dq), BF16),
            jax.ShapeDtypeStruct((t, dkv), BF16),
            jax.ShapeDtypeStruct((t, dkv), BF16),
        ],
        compiler_params=_params(1),
        name="gqa_qkv",
    )(x, g, w, bd, gains, cos, s1, s2)


def _gqa_attn_kernel(bound_ref, q_ref, k_ref, v_ref, o_ref, *, n, tk, unroll):
    tq = q_ref.shape[0]
    lane = lax.broadcasted_iota(jnp.int32, (tq, LANES), 1)
    low = lane < HEAD_DIM
    bound = bound_ref[0]

    def key_tile(j):
        start = pl.multiple_of(j * tk, tk)
        return k_ref[pl.ds(start, tk), :], v_ref[pl.ds(start, tk), :]

    for c in range(q_ref.shape[1] // LANES):
        cols = slice(c * LANES, (c + 1) * LANES)
        qc = q_ref[:, cols]
        zero = jnp.zeros_like(qc)
        qa = jnp.where(low, qc, zero)
        qb = jnp.where(low, zero, qc)

        @pl.when(bound <= SOFTMAX_SHIFT_LIMIT)
        def _():
            def body(j, carry):
                ls_a, ls_b, acc = carry
                kt, vt = key_tile(j)
                p_a = jnp.exp2(_dot_nt(qa, kt) - bound)
                p_b = jnp.exp2(_dot_nt(qb, kt) - bound)
                for i in range(tk // LANES):
                    ls_a = ls_a + p_a[:, i * LANES:(i + 1) * LANES]
                    ls_b = ls_b + p_b[:, i * LANES:(i + 1) * LANES]
                pv_a = _dot(p_a.astype(BF16), vt)
                pv_b = _dot(p_b.astype(BF16), vt)
                return ls_a, ls_b, acc + jnp.where(low, pv_a, pv_b)

            z = jnp.zeros((tq, LANES), F32)
            ls_a, ls_b, acc = lax.fori_loop(0, n // tk, body, (z, z, z), unroll=unroll)
            l_a = jnp.sum(ls_a, axis=1, keepdims=True)
            l_b = jnp.sum(ls_b, axis=1, keepdims=True)
            o_ref[:, cols] = (acc / jnp.where(low, l_a, l_b)).astype(BF16)

        @pl.when(bound > SOFTMAX_SHIFT_LIMIT)
        def _():
            def body(j, carry):
                m_a, l_a, m_b, l_b, acc = carry
                kt, vt = key_tile(j)
                s_a = _dot_nt(qa, kt)
                s_b = _dot_nt(qb, kt)
                mn_a = jnp.maximum(m_a, jnp.max(s_a, axis=-1, keepdims=True))
                mn_b = jnp.maximum(m_b, jnp.max(s_b, axis=-1, keepdims=True))
                p_a = jnp.exp2(s_a - mn_a)
                p_b = jnp.exp2(s_b - mn_b)
                al_a = jnp.exp2(m_a - mn_a)
                al_b = jnp.exp2(m_b - mn_b)
                l_a = al_a * l_a + jnp.sum(p_a, axis=-1, keepdims=True)
                l_b = al_b * l_b + jnp.sum(p_b, axis=-1, keepdims=True)
                pv_a = _dot(p_a.astype(BF16), vt)
                pv_b = _dot(p_b.astype(BF16), vt)
                acc = acc * jnp.where(low, al_a, al_b) + jnp.where(low, pv_a, pv_b)
                return mn_a, l_a, mn_b, l_b, acc

            neg = jnp.full((tq, 1), NEG_INF, F32)
            zer = jnp.zeros((tq, 1), F32)
            _, l_a, _, l_b, acc = lax.fori_loop(0, n // tk, body, (neg, zer, neg, zer, jnp.zeros((tq, LANES), F32)))
            o_ref[:, cols] = (acc / jnp.where(low, l_a, l_b)).astype(BF16)


def _gqa_attn(bound, q, k, v, b, n, tq, tk):
    t = q.shape[0]
    dq = N_HEADS * HEAD_DIM
    nq = n // tq
    pairs = A_KV_HEADS // 2
    qw = dq // pairs
    return pl.pallas_call(
        functools.partial(_gqa_attn_kernel, n=n, tk=tk, unroll=min(n // tk, GQA_UNROLL)),
        grid=(b, pairs, nq),
        in_specs=[
            pl.BlockSpec(memory_space=pltpu.SMEM),
            pl.BlockSpec((tq, qw), lambda bi, p, i: (bi * nq + i, p)),
            pl.BlockSpec((n, LANES), lambda bi, p, i: (bi, p)),
            pl.BlockSpec((n, LANES), lambda bi, p, i: (bi, p)),
        ],
        out_specs=pl.BlockSpec((tq, qw), lambda bi, p, i: (bi * nq + i, p)),
        out_shape=jax.ShapeDtypeStruct((t, dq), BF16),
        compiler_params=_params(3),
        name="gqa_attn",
    )(bound, q, k, v)


def _na_qkv_kernel(x_ref, g_ref, w_ref, bd_ref, gains_ref, q_ref, k_ref, v_ref):
    dh = N_HEADS * HEAD_DIM
    h = _rms(x_ref[...], g_ref[...]).astype(BF16)
    qkv = _dot(h, w_ref[...])
    qk = _head_norm(qkv[:, :2 * dh], bd_ref[...], gains_ref[...])
    q_ref[...] = qk[:, :dh].astype(BF16)
    k_ref[...] = qk[:, dh:].astype(BF16)
    v_ref[...] = qkv[:, 2 * dh:].astype(BF16)


def _na_qkv(x, g, w, bd, gains, tm):
    t = x.shape[0]
    dh = N_HEADS * HEAD_DIM
    full = lambda shape: pl.BlockSpec(shape, lambda i: (0, 0))
    return pl.pallas_call(
        _na_qkv_kernel,
        grid=(t // tm,),
        in_specs=[
            pl.BlockSpec((tm, D_MODEL), lambda i: (i, 0)),
            full((1, D_MODEL)),
            full((D_MODEL, 3 * dh)),
            full((MXU_DIM, MXU_DIM)),
            full((1, 2 * dh)),
        ],
        out_specs=[pl.BlockSpec((tm, dh), lambda i: (i, 0))] * 3,
        out_shape=[jax.ShapeDtypeStruct((t, dh), BF16)] * 3,
        compiler_params=_params(1),
        name="na_qkv",
    )(x, g, w, bd, gains)


def _na_attn_kernel(bound_ref, q_ref, k_ref, v_ref, bias_ref, o_ref, *, nblk, sub):
    ub = pl.program_id(2)
    blk = NA_QB * GRID_W
    win = 3 * blk
    lane = lax.broadcasted_iota(jnp.int32, (blk, LANES), 1)
    low = lane < HEAD_DIM
    bound = bound_ref[0]

    def block(s, softmax):
        u = ub * sub + s
        var = jnp.where(u == 0, 0, jnp.where(u == nblk - 1, 2, 1))
        start = pl.multiple_of(jnp.clip(u - 1, 0, nblk - 3) * blk, blk)
        kw = k_ref[pl.ds(start, win), :]
        vw = v_ref[pl.ds(start, win), :]
        qs = q_ref[s * blk:(s + 1) * blk, :]
        zero = jnp.zeros_like(qs)
        halves = []
        for hh, qm in enumerate((jnp.where(low, qs, zero), jnp.where(low, zero, qs))):
            p, l = softmax(_dot_nt(qm, kw) + bias_ref[var, hh])
            halves.append(_dot(p.astype(BF16), vw) / l)
        o_ref[s * blk:(s + 1) * blk, :] = jnp.where(low, halves[0], halves[1]).astype(BF16)

    def shifted(sc):
        p = jnp.exp2(sc - bound)
        ls = p[:, :LANES]
        for i in range(1, win // LANES):
            ls = ls + p[:, i * LANES:(i + 1) * LANES]
        return p, jnp.sum(ls, axis=1, keepdims=True)

    def running_max(sc):
        p = jnp.exp2(sc - jnp.max(sc, axis=-1, keepdims=True))
        return p, jnp.sum(p, axis=-1, keepdims=True)

    @pl.when(bound <= SOFTMAX_SHIFT_LIMIT)
    def _():
        for s in range(sub):
            block(s, shifted)

    @pl.when(bound > SOFTMAX_SHIFT_LIMIT)
    def _():
        for s in range(sub):
            block(s, running_max)


def _na_attn(bound, q, k, v, bias, b, n, sub):
    t = q.shape[0]
    dh = N_HEADS * HEAD_DIM
    blk = NA_QB * GRID_W
    nblk = n // blk
    assert nblk >= 3 and nblk % sub == 0
    nstep = nblk // sub
    tq = sub * blk
    return pl.pallas_call(
        functools.partial(_na_attn_kernel, nblk=nblk, sub=sub),
        grid=(N_HEADS // 2, b, nstep),
        in_specs=[
            pl.BlockSpec(memory_space=pltpu.SMEM),
            pl.BlockSpec((tq, LANES), lambda p, bi, i: (bi * nstep + i, p)),
            pl.BlockSpec((n, LANES), lambda p, bi, i: (bi, p)),
            pl.BlockSpec((n, LANES), lambda p, bi, i: (bi, p)),
            pl.BlockSpec((3, 2, blk, 3 * blk), lambda p, bi, i: (0, p, 0, 0)),
        ],
        out_specs=pl.BlockSpec((tq, LANES), lambda p, bi, i: (bi * nstep + i, p)),
        out_shape=jax.ShapeDtypeStruct((t, dh), BF16),
        compiler_params=_params(3),
        name="na_attn",
    )(bound, q, k, v, bias)


def _na_bias_table(rpb):
    nkr = 3 * NA_QB
    i = np.arange(NA_QB)[:, None]
    kr = np.arange(nkr)[None, :]
    variants = [(0 * i, NA_ROWS - 1), (i, NA_ROWS - 1 - NA_QB), (NA_QB + 0 * i, NA_ROWS - 1 - 2 * NA_QB)]
    row_valid = np.stack([(kr >= r0) & (kr < r0 + NA_ROWS) for r0, _ in variants])
    dr = np.stack([np.clip(kr - i + off, 0, 2 * NA_ROWS - 2) for _, off in variants])
    row_sel = (dr[..., None] == np.arange(2 * NA_ROWS - 1)).astype(np.float32)
    qc = np.arange(GRID_W)[:, None]
    kc = np.arange(GRID_W)[None, :]
    win0 = np.clip(qc - NA_KW // 2, 0, GRID_W - NA_KW)
    col_valid = (kc >= win0) & (kc < win0 + NA_KW)
    dc = np.clip(kc - qc + NA_KW - 1, 0, 2 * NA_KW - 2)
    col_sel = (dc[..., None] == np.arange(2 * NA_KW - 1)).astype(np.float32)
    hp = lax.Precision.HIGHEST
    tmp = jnp.einsum("vikd,hde->vhike", row_sel, rpb.astype(F32), precision=hp)
    tbl = jnp.einsum("vhike,qce->vhiqkc", tmp, col_sel, precision=hp)
    valid = row_valid[:, :, None, :, None] & col_valid[None, None, :, None, :]
    tbl = jnp.where(valid[:, None], tbl * LOG2_E, NEG_INF)
    return tbl.reshape(3, N_HEADS, NA_QB * GRID_W, nkr * GRID_W)


def _proj_router_kernel(o_ref, w_ref, x_ref, g_ref, rhi_ref, rlo_ref, x1_ref, hf_ref, gate_ref, afft_ref):
    tm = o_ref.shape[0]
    for r in range(tm // ROW_CHAIN):
        rows = slice(r * ROW_CHAIN, (r + 1) * ROW_CHAIN)
        x1 = x_ref[rows, :] + _dot(o_ref[rows, :], w_ref[...])
        x1_ref[rows, :] = x1
        hf = _rms(x1, g_ref[...])
        hi, lo = _split_bf16(hf)
        hf_ref[rows, :] = hi
        logits = _dot(hi, rhi_ref[...]) + (_dot(hi, rlo_ref[...]) + _dot(lo, rhi_ref[...]))
        lane = lax.broadcasted_iota(jnp.int32, logits.shape, 1)
        logits = jnp.where(lane < N_EXPERTS, logits, NEG_INF)
        m = jnp.max(logits, axis=-1, keepdims=True)
        e = jnp.exp(logits - m)
        aff = e / jnp.sum(e, axis=-1, keepdims=True)
        g_hi, g_lo = _split_bf16(aff)
        gate_ref[rows, :LANES] = g_hi
        gate_ref[rows, LANES:] = g_lo
        afft_ref[:, rows] = aff.T[:N_EXPERTS, :]


def _proj_router(o, w_o, x, g, r_hi, r_lo, tm):
    t = x.shape[0]
    full = lambda shape: pl.BlockSpec(shape, lambda i: (0, 0))
    return pl.pallas_call(
        _proj_router_kernel,
        grid=(t // tm,),
        in_specs=[
            pl.BlockSpec((tm, D_MODEL), lambda i: (i, 0)),
            full((D_MODEL, D_MODEL)),
            pl.BlockSpec((tm, D_MODEL), lambda i: (i, 0)),
            full((1, D_MODEL)),
            full((D_MODEL, LANES)),
            full((D_MODEL, LANES)),
        ],
        out_specs=[
            pl.BlockSpec((tm, D_MODEL), lambda i: (i, 0)),
            pl.BlockSpec((tm, D_MODEL), lambda i: (i, 0)),
            pl.BlockSpec((tm, GATE_LANES), lambda i: (i, 0)),
            pl.BlockSpec((N_EXPERTS, tm), lambda i: (0, i)),
        ],
        out_shape=[
            jax.ShapeDtypeStruct((t, D_MODEL), F32),
            jax.ShapeDtypeStruct((t, D_MODEL), BF16),
            jax.ShapeDtypeStruct((t, GATE_LANES), BF16),
            jax.ShapeDtypeStruct((N_EXPERTS, t), F32),
        ],
        compiler_params=_params(1),
        name="proj_router",
    )(o, w_o, x, g, r_hi, r_lo)


def _select_kernel(aff_ref, upper_ref, lower_ref, pos_ref, offs_ref, *, cap):
    aff = aff_ref[...]
    ne, nr, _ = aff.shape

    def count(mask):
        return jnp.sum(jnp.sum(jnp.where(mask, 1.0, 0.0), axis=2, keepdims=True), axis=1, keepdims=True)

    def bisect(i, bits):
        cand = bits | jnp.left_shift(jnp.int32(1), 30 - i)
        return jnp.where(count(aff >= pltpu.bitcast(cand, F32)) >= cap, cand, bits)

    thr = pltpu.bitcast(lax.fori_loop(0, 31, bisect, jnp.zeros((ne, 1, 1), jnp.int32)), F32)
    above = aff > thr
    tied = aff == thr
    need = cap - count(above)

    ones = jnp.ones((LANES, LANES), BF16)

    def prefix(mask):
        mb = jnp.where(mask, 1.0, 0.0).astype(BF16).reshape(ne * nr, LANES)
        incl = _dot(mb, upper_ref[...]).reshape(ne, nr, LANES)
        tot = _dot(mb, ones).astype(BF16).reshape(ne, nr, LANES)
        offs = jnp.stack([_dot(lower_ref[...], tot[e]) for e in range(ne)], axis=0)
        return offs + incl - mb.astype(F32).reshape(ne, nr, LANES), offs

    tie_rank, _ = prefix(tied)
    sel = above | (tied & (tie_rank < need))
    pos, offs = prefix(sel)
    pos_ref[...] = jnp.where(sel, pos, -1.0).astype(jnp.int32)
    offs_ref[...] = offs.astype(jnp.int32)


def _select(afft, cap):
    ne, n = afft.shape
    nr = n // LANES
    aff3 = afft.reshape(ne, nr, LANES)
    upper = jnp.asarray(np.triu(np.ones((LANES, LANES), np.float32)), BF16)
    lower = jnp.asarray(np.tril(np.ones((nr, nr), np.float32), -1), BF16)
    blk3 = pl.BlockSpec((ne, nr, LANES), lambda i: (0, 0, 0))
    return pl.pallas_call(
        functools.partial(_select_kernel, cap=cap),
        grid=(1,),
        in_specs=[blk3, pl.BlockSpec((LANES, LANES), lambda i: (0, 0)), pl.BlockSpec((nr, nr), lambda i: (0, 0))],
        out_specs=[blk3, blk3],
        out_shape=[jax.ShapeDtypeStruct((ne, nr, LANES), jnp.int32)] * 2,
        compiler_params=_params(1),
        name="ec_select",
    )(aff3, upper, lower)


def _align_down(x):
    return lax.shift_left(lax.shift_right_logical(x, ROW_ALIGN.bit_length() - 1), ROW_ALIGN.bit_length() - 1)


def _window_info(st_ref, e, t, nt):
    start = st_ref[e * (nt + 1) + t]
    end = st_ref[e * (nt + 1) + t + 1]
    base = _align_down(start)
    return base, end, lax.shift_right_logical(end - base, MOE_WIN.bit_length() - 1) + 1


def _dispatch_kernel(st_ref, hf_ref, gate_ref, pos_ref, xe_ref, stage, extra, carry, sems, xsem, *, nt, cap):
    t = pl.program_id(0)
    ne, tt = pos_ref.shape
    slot = t % 2
    hf = jnp.concatenate([hf_ref[...], gate_ref[...]], axis=1)
    rows = lax.broadcasted_iota(jnp.int32, (MOE_WIN, tt), 0)
    info = [_window_info(st_ref, e, t, nt) for e in range(ne)]

    @pl.when(t == 0)
    def _():
        carry[...] = jnp.zeros_like(carry)
        extra[...] = jnp.zeros_like(extra)
        pads = [pltpu.make_async_copy(extra.at[pl.ds(e * MOE_WIN, MOE_WIN), :], xe_ref.at[e, pl.ds(cap, MOE_WIN), :], xsem)
                for e in range(ne)]
        for cp in pads:
            cp.start()
        for cp in pads:
            cp.wait()

    def gathered(k):
        parts = []
        for e in range(ne):
            rel = pos_ref[e:e + 1, :] - (info[e][0] + k * MOE_WIN)
            parts.append(jnp.where(rel == rows, 1.0, 0.0).astype(BF16))
        return _dot(jnp.concatenate(parts, axis=0), hf).astype(BF16)

    def window(buf, e, k, sem):
        row = pl.multiple_of(info[e][0] + k * MOE_WIN, ROW_ALIGN)
        return pltpu.make_async_copy(buf.at[pl.ds(e * MOE_WIN, MOE_WIN), :], xe_ref.at[e, pl.ds(row, MOE_WIN), :], sem)

    stage[slot] = gathered(0)
    for e in range(ne):
        base, end, nwin = info[e]
        head = pl.ds(e * MOE_WIN, ROW_ALIGN)
        stage[slot, head, :] = stage[slot, head, :] + carry[e]

        @pl.when(nwin == 1)
        def _():
            tail = pl.multiple_of(e * MOE_WIN + _align_down(end) - base, ROW_ALIGN)
            carry[e] = stage[slot, pl.ds(tail, ROW_ALIGN), :]

        @pl.when(t > 0)
        def _():
            window(stage.at[1 - slot], e, 0, sems.at[e]).wait()

        window(stage.at[slot], e, 0, sems.at[e]).start()

    def more(k, c):
        extra[...] = gathered(k)
        for e in range(ne):
            base, end, nwin = info[e]

            @pl.when(k < nwin)
            def _():
                cp = window(extra, e, k, xsem)
                cp.start()
                cp.wait()

            @pl.when(k == nwin - 1)
            def _():
                tail = pl.multiple_of(e * MOE_WIN + _align_down(end) - base - k * MOE_WIN, ROW_ALIGN)
                carry[e] = extra[pl.ds(tail, ROW_ALIGN), :]
        return c

    lax.fori_loop(1, functools.reduce(jnp.maximum, [i[2] for i in info]), more, 0)

    @pl.when(t == nt - 1)
    def _():
        for e in range(ne):
            window(stage.at[slot], e, 0, sems.at[e]).wait()


def _dispatch(starts, hf, gates, posm, cap, tt):
    ne, n = posm.shape
    nt = n // tt
    width = D_MODEL + GATE_LANES
    grid_spec = pltpu.PrefetchScalarGridSpec(
        num_scalar_prefetch=1,
        grid=(nt,),
        in_specs=[
            pl.BlockSpec((tt, D_MODEL), lambda t, st: (t, 0)),
            pl.BlockSpec((tt, GATE_LANES), lambda t, st: (t, 0)),
            pl.BlockSpec((ne, tt), lambda t, st: (0, t)),
        ],
        out_specs=pl.BlockSpec(memory_space=pl.ANY),
        scratch_shapes=[
            pltpu.VMEM((2, ne * MOE_WIN, width), BF16),
            pltpu.VMEM((ne * MOE_WIN, width), BF16),
            pltpu.VMEM((ne, ROW_ALIGN, width), BF16),
            pltpu.SemaphoreType.DMA((ne,)),
            pltpu.SemaphoreType.DMA(()),
        ],
    )
    return pl.pallas_call(
        functools.partial(_dispatch_kernel, nt=nt, cap=cap),
        grid_spec=grid_spec,
        out_shape=jax.ShapeDtypeStruct((ne, cap + MOE_WIN, width), BF16),
        compiler_params=_params(1),
        name="ec_dispatch",
    )(starts, hf, gates, posm)


def _combine_kernel(st_ref, y_ref, posc_ref, x_ref, out_ref, ybuf, extra, sems, xsem, *, nt, cap):
    t = pl.program_id(0)
    tt, ne = posc_ref.shape
    slot = t % 2
    last = cap - MOE_WIN

    def main_row(tile, e):
        return jnp.minimum(_window_info(st_ref, e, tile, nt)[0], last)

    def window(row, e, buf, sem):
        src = y_ref.at[e, pl.ds(pl.multiple_of(row, ROW_ALIGN), MOE_WIN), :]
        return pltpu.make_async_copy(src, buf.at[pl.ds(e * MOE_WIN, MOE_WIN), :], sem)

    @pl.when(t == 0)
    def _():
        extra[...] = jnp.zeros_like(extra)
        for e in range(ne):
            window(main_row(0, e), e, ybuf.at[0], sems.at[0, e]).start()

    @pl.when(t + 1 < nt)
    def _():
        for e in range(ne):
            window(main_row(t + 1, e), e, ybuf.at[1 - slot], sems.at[1 - slot, e]).start()

    row0 = [main_row(t, e) for e in range(ne)]
    for e in range(ne):
        window(row0[e], e, ybuf.at[slot], sems.at[slot, e]).wait()

    def onehot(rows, floors, toks=slice(None)):
        parts = []
        for e in range(ne):
            pos = posc_ref[toks, e:e + 1]
            hit = (pos - rows[e]) == lax.broadcasted_iota(jnp.int32, (pos.shape[0], MOE_WIN), 1)
            if floors is not None:
                hit = jnp.logical_and(hit, pos >= floors[e])
            parts.append(jnp.where(hit, 1.0, 0.0).astype(BF16))
        return jnp.concatenate(parts, axis=1)

    y = ybuf[slot]
    for r in range(tt // ROW_CHAIN):
        rows = slice(r * ROW_CHAIN, (r + 1) * ROW_CHAIN)
        out_ref[rows, :] = x_ref[rows, :] + _dot(onehot(row0, None, rows), y)

    ends = [_window_info(st_ref, e, t, nt)[1] for e in range(ne)]
    nwin = [lax.shift_right_logical(jnp.maximum(ends[e] - row0[e], 1) - 1, MOE_WIN.bit_length() - 1) + 1 for e in range(ne)]

    def more(k, c):
        floors = [row0[e] + k * MOE_WIN for e in range(ne)]
        rows = [jnp.minimum(f, last) for f in floors]
        for e in range(ne):
            @pl.when(k < nwin[e])
            def _():
                cp = window(rows[e], e, extra, xsem)
                cp.start()
                cp.wait()
        out_ref[...] += _dot(onehot(rows, floors), extra[...])
        return c

    lax.fori_loop(1, functools.reduce(jnp.maximum, nwin), more, 0)


def _combine(starts, y, posc, x, tt):
    n, ne = posc.shape
    nt = n // tt
    grid_spec = pltpu.PrefetchScalarGridSpec(
        num_scalar_prefetch=1,
        grid=(nt,),
        in_specs=[
            pl.BlockSpec(memory_space=pl.ANY),
            pl.BlockSpec((tt, ne), lambda t, st: (t, 0)),
            pl.BlockSpec((tt, D_MODEL), lambda t, st: (t, 0)),
        ],
        out_specs=pl.BlockSpec((tt, D_MODEL), lambda t, st: (t, 0)),
        scratch_shapes=[
            pltpu.VMEM((2, ne * MOE_WIN, D_MODEL), BF16),
            pltpu.VMEM((ne * MOE_WIN, D_MODEL), BF16),
            pltpu.SemaphoreType.DMA((2, ne)),
            pltpu.SemaphoreType.DMA(()),
        ],
    )
    return pl.pallas_call(
        functools.partial(_combine_kernel, nt=nt, cap=y.shape[1]),
        grid_spec=grid_spec,
        out_shape=jax.ShapeDtypeStruct((n, D_MODEL), F32),
        compiler_params=_params(1),
        name="ec_combine",
    )(starts, y, posc, x)


def _ffn_kernel(x_ref, wg_ref, wu_ref, wd_ref, y_ref, *, fc):
    x = x_ref[0, :, :D_MODEL]
    parts = x_ref[0, :, D_MODEL:].astype(F32)
    lane = lax.broadcasted_iota(jnp.int32, parts.shape, 1)
    gate = jnp.sum(jnp.where((lane % LANES) == pl.program_id(0), parts, 0.0), axis=1, keepdims=True)
    acc = jnp.zeros(y_ref.shape[1:], F32)
    for c in range(EXPERT_FF // fc):
        sl = slice(c * fc, (c + 1) * fc)
        a = _dot(x, wg_ref[0, :, sl])
        u = _dot(x, wu_ref[0, :, sl])
        hid = (a * (1.0 / (1.0 + jnp.exp(-a))) * u).astype(BF16)
        acc = acc + _dot(hid, wd_ref[0, sl, :])
    y_ref[0] = (acc * gate).astype(BF16)


def _ffn(xe, wg, wu, wd, cap, bm, fc):
    ne = wg.shape[0]
    return pl.pallas_call(
        functools.partial(_ffn_kernel, fc=fc),
        grid=(ne, cap // bm),
        in_specs=[
            pl.BlockSpec((1, bm, D_MODEL + GATE_LANES), lambda e, j: (e, j, 0)),
            pl.BlockSpec((1, D_MODEL, EXPERT_FF), lambda e, j: (e, 0, 0)),
            pl.BlockSpec((1, D_MODEL, EXPERT_FF), lambda e, j: (e, 0, 0)),
            pl.BlockSpec((1, EXPERT_FF, D_MODEL), lambda e, j: (e, 0, 0)),
        ],
        out_specs=pl.BlockSpec((1, bm, D_MODEL), lambda e, j: (e, j, 0)),
        out_shape=jax.ShapeDtypeStruct((ne, cap, D_MODEL), BF16),
        compiler_params=_params(2),
        name="ec_ffn",
    )(xe, wg, wu, wd)


def _tile(n, pref):
    return pref if n % pref == 0 else n


def _rope_tables(n):
    tok = jnp.arange(n, dtype=jnp.int32)
    pos = jnp.stack([tok // GRID_W, tok % GRID_W], axis=-1).astype(F32)
    inv = ROPE_THETA ** (-jnp.arange(0, ROPE_AXIS_DIM, 2, dtype=F32) / ROPE_AXIS_DIM)
    ang = pos[:, :, None] * inv
    cos, sin = jnp.cos(ang), jnp.sin(ang)
    half = ROPE_AXIS_DIM // 2
    cos_h = jnp.concatenate([cos, cos], axis=-1).reshape(n, HEAD_DIM)
    zeros = jnp.zeros_like(sin)
    s1_h = jnp.concatenate([zeros, sin], axis=-1).reshape(n, HEAD_DIM)
    s2_h = jnp.concatenate([-sin, zeros], axis=-1).reshape(n, HEAD_DIM)
    del half
    rep = lambda a: jnp.concatenate([a, a], axis=-1)
    return rep(cos_h), rep(s1_h), rep(s2_h)


def _block_diag_ones():
    idx = np.arange(MXU_DIM) // HEAD_DIM
    return jnp.asarray((idx[:, None] == idx[None, :]).astype(np.float32), BF16)


_A_GROUP = N_HEADS // A_KV_HEADS
_Q_PERM = np.array([kvp * 2 * _A_GROUP + half * _A_GROUP + g
                    for kvp in range(A_KV_HEADS // 2) for g in range(_A_GROUP) for half in range(2)])


def _perm_head_cols(w):
    d = w.shape[0]
    return w.reshape(d, N_HEADS, HEAD_DIM)[:, _Q_PERM].reshape(d, N_HEADS * HEAD_DIM)


def _moe(x1, hf, gates, afft, wg, wu, wd):
    n = x1.shape[0]
    cap = max(1, CAPACITY_FACTOR * n // N_EXPERTS)
    tt = _tile(n, MOE_TILE)
    bm = _tile(cap, 512)
    pos3, offs3 = _select(afft, cap)
    starts = jnp.concatenate([offs3[:, ::tt // LANES, 0], jnp.full((N_EXPERTS, 1), cap, jnp.int32)], axis=1)
    starts = starts.reshape(-1)
    posm = pos3.reshape(N_EXPERTS, n)
    xe = _dispatch(starts, hf, gates, posm, cap, tt)
    y = _ffn(xe, wg, wu, wd, cap, bm, 512)
    return _combine(starts, y, posm.T, x1, tt)


def _trunk(x, p):
    b, n, _ = x.shape
    t = b * n
    tm = _tile(n, 512)
    x2 = x.reshape(t, D_MODEL)
    for i in range(p["depth"]):
        j = i // 2
        g_mix = p["norm_mix"][i].reshape(1, D_MODEL)
        if i % 2 == 0:
            q, k, v = _gqa_qkv(x2, g_mix, p["a_w_qkv"][j], p["bd"], p["a_gains"][j], *p["rope"][n], n, tm)
            o = _gqa_attn(p["a_bound"][j], q, k, v, b, n, _tile(n, 256), _tile(n, 512))
            w_o = p["a_w_o"][j]
        else:
            q, k, v = _na_qkv(x2, g_mix, p["b_w_qkv"][j], p["bd"], p["b_gains"][j], tm)
            o = _na_attn(p["b_bound"][j], q, k, v, p["b_bias"][j], b, n, 4)
            w_o = p["b_w_o"][j]
        x1, hf, gates, afft = _proj_router(o, w_o, x2, p["norm_ffn"][i].reshape(1, D_MODEL),
                                          p["r_hi"][i], p["r_lo"][i], tm)
        x2 = _moe(x1, hf, gates, afft, p["wg"][i], p["wu"][i], p["wd"][i])
    return x2.reshape(b, n, D_MODEL)


def kernel(x_prompt, x_sample, norm_mix, norm_ffn, a_w_qkv, a_q_norm, a_k_norm, a_w_o, b_w_qkv, b_q_norm, b_k_norm,
           b_rpb, b_w_o, moe_router, moe_w_gate, moe_w_up, moe_w_down):
    depth = norm_mix.shape[0]
    dq = N_HEADS * HEAD_DIM
    n_a, n_b = a_w_qkv.shape[0], b_w_qkv.shape[0]
    a_w = jnp.concatenate([jnp.stack([_perm_head_cols(a_w_qkv[j, :, :dq]) for j in range(n_a)]),
                           a_w_qkv[:, :, dq:]], axis=-1).astype(BF16)
    a_wo = jnp.stack([a_w_o[j].reshape(N_HEADS, HEAD_DIM, D_MODEL)[_Q_PERM].reshape(dq, D_MODEL)
                      for j in range(n_a)]).astype(BF16)
    a_gains = jnp.concatenate([jnp.tile(a_q_norm, (1, N_HEADS)) * Q_GAIN_SCALE,
                               jnp.tile(a_k_norm, (1, A_KV_HEADS))], axis=-1).reshape(n_a, 1, -1)
    b_gains = jnp.concatenate([jnp.tile(b_q_norm, (1, N_HEADS)) * Q_GAIN_SCALE,
                               jnp.tile(b_k_norm, (1, N_HEADS))], axis=-1).reshape(n_b, 1, -1)
    r_pad = jnp.pad(moe_router, ((0, 0), (0, 0), (0, LANES - N_EXPERTS)))
    r_hi = r_pad.astype(BF16)
    r_lo = (r_pad - r_hi.astype(F32)).astype(BF16)
    a_bound = (HEAD_DIM * 1.01) * (jnp.max(jnp.abs(a_gains[:, 0, :dq]), axis=-1, keepdims=True)
                                   * jnp.max(jnp.abs(a_gains[:, 0, dq:]), axis=-1, keepdims=True))
    b_bound = (HEAD_DIM * 1.01) * (jnp.max(jnp.abs(b_gains[:, 0, :dq]), axis=-1, keepdims=True)
                                   * jnp.max(jnp.abs(b_gains[:, 0, dq:]), axis=-1, keepdims=True))
    b_bound = b_bound + LOG2_E * jnp.max(jnp.abs(b_rpb), axis=(1, 2, 3)).reshape(n_b, 1)
    p = {
        "depth": depth, "a_bound": a_bound, "b_bound": b_bound,
        "norm_mix": norm_mix, "norm_ffn": norm_ffn,
        "a_w_qkv": a_w, "a_w_o": a_wo, "a_gains": a_gains,
        "b_w_qkv": b_w_qkv.astype(BF16), "b_w_o": b_w_o.astype(BF16), "b_gains": b_gains,
        "b_bias": jnp.stack([_na_bias_table(b_rpb[j]) for j in range(n_b)]),
        "bd": _block_diag_ones(),
        "r_hi": r_hi, "r_lo": r_lo,
        "wg": moe_w_gate.astype(BF16), "wu": moe_w_up.astype(BF16), "wd": moe_w_down.astype(BF16),
        "rope": {n: _rope_tables(n) for n in {x_prompt.shape[1], x_sample.shape[1]}},
    }
    return _trunk(x_prompt, p), _trunk(x_sample, p)
```

```python
import functools

import jax
import jax.numpy as jnp
import numpy as np
from jax import lax
from jax.experimental import pallas as pl
from jax.experimental.pallas import tpu as pltpu

D_MODEL = 1024
HEAD_DIM = 64
N_HEADS = 16
A_KV_HEADS = 4
GRID_W = 64
ROPE_THETA = 10000.0
ROPE_AXIS_DIM = HEAD_DIM // 2
NA_ROWS = 8
NA_KW = 16
NA_QB = 4
N_EXPERTS = 16
CAPACITY_FACTOR = 2
EXPERT_FF = 2048
RMS_EPS = 1e-6
NEG_INF = -1e30
ATTN_SCALE = HEAD_DIM ** -0.5
LOG2_E = float(np.log2(np.e))
Q_GAIN_SCALE = ATTN_SCALE * LOG2_E
SOFTMAX_SHIFT_LIMIT = 56.0
GQA_UNROLL = 8

LANES = 128
MXU_DIM = 256
VMEM_LIMIT = 48 * 1024 * 1024
ROW_ALIGN = 16
MOE_WIN = 128
MOE_TILE = 512
GATE_LANES = 2 * LANES
FFN_ROWS = 1024
ROW_CHAIN = 256

BF16 = jnp.bfloat16
F32 = jnp.float32


def _params(n_axes, vmem=VMEM_LIMIT):
    return pltpu.CompilerParams(dimension_semantics=("arbitrary",) * n_axes, vmem_limit_bytes=vmem)


def _dot(a, b):
    return jnp.dot(a, b, preferred_element_type=F32)


def _dot_nt(a, b):
    return lax.dot_general(a, b, (((1,), (1,)), ((), ())), preferred_element_type=F32)


def _rms(x, g):
    ms = jnp.mean(x * x, axis=-1, keepdims=True)
    return x * lax.rsqrt(ms + RMS_EPS) * g


def _split_bf16(x):
    hi = x.astype(BF16)
    lo = (x - hi.astype(F32)).astype(BF16)
    return hi, lo


def _head_norm(y, bd, gains):
    sq = y * y
    hi, lo = _split_bf16(sq)
    w = y.shape[1]
    parts = []
    for c in range(w // MXU_DIM):
        sl = slice(c * MXU_DIM, (c + 1) * MXU_DIM)
        parts.append(_dot(hi[:, sl], bd) + _dot(lo[:, sl], bd))
    ss = jnp.concatenate(parts, axis=1)
    return y * lax.rsqrt(ss * (1.0 / HEAD_DIM) + RMS_EPS) * gains


def _gqa_qkv_kernel(x_ref, g_ref, w_ref, bd_ref, gains_ref, cos_ref, s1_ref, s2_ref, q_ref, k_ref, v_ref):
    dq = N_HEADS * HEAD_DIM
    dkv = A_KV_HEADS * HEAD_DIM
    half = ROPE_AXIS_DIM // 2
    for r in range(x_ref.shape[0] // ROW_CHAIN):
        rows = slice(r * ROW_CHAIN, (r + 1) * ROW_CHAIN)
        h = _rms(x_ref[rows, :], g_ref[...]).astype(BF16)
        qkv = _dot(h, w_ref[...])
        qk = _head_norm(qkv[:, :dq + dkv], bd_ref[...], gains_ref[...])
        cos, s1, s2 = cos_ref[rows, :], s1_ref[rows, :], s2_ref[rows, :]
        outs = []
        for c in range((dq + dkv) // LANES):
            xc = qk[:, c * LANES:(c + 1) * LANES]
            outs.append(xc * cos + pltpu.roll(xc, half, 1) * s1 + pltpu.roll(xc, LANES - half, 1) * s2)
        roped = jnp.concatenate(outs, axis=1)
        q_ref[rows, :] = roped[:, :dq].astype(BF16)
        k_ref[rows, :] = roped[:, dq:].astype(BF16)
        v_ref[rows, :] = qkv[:, dq + dkv:].astype(BF16)


def _gqa_qkv(x, g, w, bd, gains, cos, s1, s2, n, tm):
    t = x.shape[0]
    dq = N_HEADS * HEAD_DIM
    dkv = A_KV_HEADS * HEAD_DIM
    npos = n // tm
    full = lambda shape: pl.BlockSpec(shape, lambda i: (0, 0))
    return pl.pallas_call(
        _gqa_qkv_kernel,
        grid=(t // tm,),
        in_specs=[
            pl.BlockSpec((tm, D_MODEL), lambda i: (i, 0)),
            full((1, D_MODEL)),
            full((D_MODEL, dq + 2 * dkv)),
            full((MXU_DIM, MXU_DIM)),
            full((1, dq + dkv)),
            pl.BlockSpec((tm, LANES), lambda i: (i % npos, 0)),
            pl.BlockSpec((tm, LANES), lambda i: (i % npos, 0)),
            pl.BlockSpec((tm, LANES), lambda i: (i % npos, 0)),
        ],
        out_specs=[
            pl.BlockSpec((tm, dq), lambda i: (i, 0)),
            pl.BlockSpec((tm, dkv), lambda i: (i, 0)),
            pl.BlockSpec((tm, dkv), lambda i: (i, 0)),
        ],
        out_shape=[
            jax.ShapeDtypeStruct((t, dq), BF16),
            jax.ShapeDtypeStruct((t, dkv), BF16),
            jax.ShapeDtypeStruct((t, dkv), BF16),
        ],
        compiler_params=_params(1),
        name="gqa_qkv",
    )(x, g, w, bd, gains, cos, s1, s2)


def _gqa_attn_kernel(bound_ref, q_ref, k_ref, v_ref, o_ref, *, n, tk, unroll):
    tq = q_ref.shape[0]
    lane = lax.broadcasted_iota(jnp.int32, (tq, LANES), 1)
    low = lane < HEAD_DIM
    bound = bound_ref[0]

    def key_tile(j):
        start = pl.multiple_of(j * tk, tk)
        return k_ref[pl.ds(start, tk), :], v_ref[pl.ds(start, tk), :]

    nchunk = q_ref.shape[1] // LANES

    def masked(c):
        qc = q_ref[:, c * LANES:(c + 1) * LANES]
        zero = jnp.zeros_like(qc)
        return jnp.where(low, qc, zero), jnp.where(low, zero, qc)

    @pl.when(bound <= SOFTMAX_SHIFT_LIMIT)
    def _():
        q_all = jnp.concatenate([h for c in range(nchunk) for h in masked(c)], axis=0)

        def body(j, carry):
            ls, acc = carry
            kt, vt = key_tile(j)
            p = jnp.exp2(_dot_nt(q_all, kt) - bound)
            for i in range(tk // LANES):
                ls = ls + p[:, i * LANES:(i + 1) * LANES]
            return ls, acc + _dot(p.astype(BF16), vt)

        z = jnp.zeros((2 * nchunk * tq, LANES), F32)
        ls, acc = lax.fori_loop(0, n // tk, body, (z, z), unroll=unroll)
        o = acc / jnp.sum(ls, axis=1, keepdims=True)
        for c in range(nchunk):
            o_a = o[2 * c * tq:(2 * c + 1) * tq]
            o_b = o[(2 * c + 1) * tq:(2 * c + 2) * tq]
            o_ref[:, c * LANES:(c + 1) * LANES] = jnp.where(low, o_a, o_b).astype(BF16)

    @pl.when(bound > SOFTMAX_SHIFT_LIMIT)
    def _():
        for c in range(nchunk):
            qa, qb = masked(c)

            def body(j, carry):
                m_a, l_a, m_b, l_b, acc = carry
                kt, vt = key_tile(j)
                s_a = _dot_nt(qa, kt)
                s_b = _dot_nt(qb, kt)
                mn_a = jnp.maximum(m_a, jnp.max(s_a, axis=-1, keepdims=True))
                mn_b = jnp.maximum(m_b, jnp.max(s_b, axis=-1, keepdims=True))
                p_a = jnp.exp2(s_a - mn_a)
                p_b = jnp.exp2(s_b - mn_b)
                al_a = jnp.exp2(m_a - mn_a)
                al_b = jnp.exp2(m_b - mn_b)
                l_a = al_a * l_a + jnp.sum(p_a, axis=-1, keepdims=True)
                l_b = al_b * l_b + jnp.sum(p_b, axis=-1, keepdims=True)
                pv_a = _dot(p_a.astype(BF16), vt)
                pv_b = _dot(p_b.astype(BF16), vt)
                acc = acc * jnp.where(low, al_a, al_b) + jnp.where(low, pv_a, pv_b)
                return mn_a, l_a, mn_b, l_b, acc

            neg = jnp.full((tq, 1), NEG_INF, F32)
            zer = jnp.zeros((tq, 1), F32)
            _, l_a, _, l_b, acc = lax.fori_loop(0, n // tk, body, (neg, zer, neg, zer, jnp.zeros((tq, LANES), F32)))
            o_ref[:, c * LANES:(c + 1) * LANES] = (acc / jnp.where(low, l_a, l_b)).astype(BF16)


def _gqa_attn(bound, q, k, v, b, n, tq, tk):
    t = q.shape[0]
    dq = N_HEADS * HEAD_DIM
    nq = n // tq
    pairs = A_KV_HEADS // 2
    qw = dq // pairs
    return pl.pallas_call(
        functools.partial(_gqa_attn_kernel, n=n, tk=tk, unroll=min(n // tk, GQA_UNROLL)),
        grid=(b, pairs, nq),
        in_specs=[
            pl.BlockSpec(memory_space=pltpu.SMEM),
            pl.BlockSpec((tq, qw), lambda bi, p, i: (bi * nq + i, p)),
            pl.BlockSpec((n, LANES), lambda bi, p, i: (bi, p)),
            pl.BlockSpec((n, LANES), lambda bi, p, i: (bi, p)),
        ],
        out_specs=pl.BlockSpec((tq, qw), lambda bi, p, i: (bi * nq + i, p)),
        out_shape=jax.ShapeDtypeStruct((t, dq), BF16),
        compiler_params=_params(3),
        name="gqa_attn",
    )(bound, q, k, v)


def _na_qkv_kernel(x_ref, g_ref, w_ref, bd_ref, gains_ref, q_ref, k_ref, v_ref):
    dh = N_HEADS * HEAD_DIM
    h = _rms(x_ref[...], g_ref[...]).astype(BF16)
    qkv = _dot(h, w_ref[...])
    qk = _head_norm(qkv[:, :2 * dh], bd_ref[...], gains_ref[...])
    q_ref[...] = qk[:, :dh].astype(BF16)
    k_ref[...] = qk[:, dh:].astype(BF16)
    v_ref[...] = qkv[:, 2 * dh:].astype(BF16)


def _na_qkv(x, g, w, bd, gains, tm):
    t = x.shape[0]
    dh = N_HEADS * HEAD_DIM
    full = lambda shape: pl.BlockSpec(shape, lambda i: (0, 0))
    return pl.pallas_call(
        _na_qkv_kernel,
        grid=(t // tm,),
        in_specs=[
            pl.BlockSpec((tm, D_MODEL), lambda i: (i, 0)),
            full((1, D_MODEL)),
            full((D_MODEL, 3 * dh)),
            full((MXU_DIM, MXU_DIM)),
            full((1, 2 * dh)),
        ],
        out_specs=[pl.BlockSpec((tm, dh), lambda i: (i, 0))] * 3,
        out_shape=[jax.ShapeDtypeStruct((t, dh), BF16)] * 3,
        compiler_params=_params(1),
        name="na_qkv",
    )(x, g, w, bd, gains)


def _na_attn_kernel(bound_ref, q_ref, k_ref, v_ref, bias_ref, o_ref, *, nblk, sub):
    ub = pl.program_id(2)
    blk = NA_QB * GRID_W
    win = 3 * blk
    lane = lax.broadcasted_iota(jnp.int32, (blk, LANES), 1)
    low = lane < HEAD_DIM
    bound = bound_ref[0]

    def block(s, softmax):
        u = ub * sub + s
        var = jnp.where(u == 0, 0, jnp.where(u == nblk - 1, 2, 1))
        start = pl.multiple_of(jnp.clip(u - 1, 0, nblk - 3) * blk, blk)
        kw = k_ref[pl.ds(start, win), :]
        vw = v_ref[pl.ds(start, win), :]
        qs = q_ref[s * blk:(s + 1) * blk, :]
        zero = jnp.zeros_like(qs)
        halves = []
        for hh, qm in enumerate((jnp.where(low, qs, zero), jnp.where(low, zero, qs))):
            p, l = softmax(_dot_nt(qm, kw) + bias_ref[var, hh])
            halves.append(_dot(p.astype(BF16), vw) / l)
        o_ref[s * blk:(s + 1) * blk, :] = jnp.where(low, halves[0], halves[1]).astype(BF16)

    def shifted(sc):
        p = jnp.exp2(sc - bound)
        ls = p[:, :LANES]
        for i in range(1, win // LANES):
            ls = ls + p[:, i * LANES:(i + 1) * LANES]
        return p, jnp.sum(ls, axis=1, keepdims=True)

    def running_max(sc):
        p = jnp.exp2(sc - jnp.max(sc, axis=-1, keepdims=True))
        return p, jnp.sum(p, axis=-1, keepdims=True)

    @pl.when(bound <= SOFTMAX_SHIFT_LIMIT)
    def _():
        for s in range(sub):
            block(s, shifted)

    @pl.when(bound > SOFTMAX_SHIFT_LIMIT)
    def _():
        for s in range(sub):
            block(s, running_max)


def _na_attn(bound, q, k, v, bias, b, n, sub):
    t = q.shape[0]
    dh = N_HEADS * HEAD_DIM
    blk = NA_QB * GRID_W
    nblk = n // blk
    assert nblk >= 3 and nblk % sub == 0
    nstep = nblk // sub
    tq = sub * blk
    return pl.pallas_call(
        functools.partial(_na_attn_kernel, nblk=nblk, sub=sub),
        grid=(N_HEADS // 2, b, nstep),
        in_specs=[
            pl.BlockSpec(memory_space=pltpu.SMEM),
            pl.BlockSpec((tq, LANES), lambda p, bi, i: (bi * nstep + i, p)),
            pl.BlockSpec((n, LANES), lambda p, bi, i: (bi, p)),
            pl.BlockSpec((n, LANES), lambda p, bi, i: (bi, p)),
            pl.BlockSpec((3, 2, blk, 3 * blk), lambda p, bi, i: (0, p, 0, 0)),
        ],
        out_specs=pl.BlockSpec((tq, LANES), lambda p, bi, i: (bi * nstep + i, p)),
        out_shape=jax.ShapeDtypeStruct((t, dh), BF16),
        compiler_params=_params(3),
        name="na_attn",
    )(bound, q, k, v, bias)


def _na_bias_table(rpb):
    nkr = 3 * NA_QB
    i = np.arange(NA_QB)[:, None]
    kr = np.arange(nkr)[None, :]
    variants = [(0 * i, NA_ROWS - 1), (i, NA_ROWS - 1 - NA_QB), (NA_QB + 0 * i, NA_ROWS - 1 - 2 * NA_QB)]
    row_valid = np.stack([(kr >= r0) & (kr < r0 + NA_ROWS) for r0, _ in variants])
    dr = np.stack([np.clip(kr - i + off, 0, 2 * NA_ROWS - 2) for _, off in variants])
    row_sel = (dr[..., None] == np.arange(2 * NA_ROWS - 1)).astype(np.float32)
    qc = np.arange(GRID_W)[:, None]
    kc = np.arange(GRID_W)[None, :]
    win0 = np.clip(qc - NA_KW // 2, 0, GRID_W - NA_KW)
    col_valid = (kc >= win0) & (kc < win0 + NA_KW)
    dc = np.clip(kc - qc + NA_KW - 1, 0, 2 * NA_KW - 2)
    col_sel = (dc[..., None] == np.arange(2 * NA_KW - 1)).astype(np.float32)
    hp = lax.Precision.HIGHEST
    tmp = jnp.einsum("vikd,hde->vhike", row_sel, rpb.astype(F32), precision=hp)
    tbl = jnp.einsum("vhike,qce->vhiqkc", tmp, col_sel, precision=hp)
    valid = row_valid[:, :, None, :, None] & col_valid[None, None, :, None, :]
    tbl = jnp.where(valid[:, None], tbl * LOG2_E, NEG_INF)
    return tbl.reshape(3, N_HEADS, NA_QB * GRID_W, nkr * GRID_W)


def _proj_router_kernel(o_ref, w_ref, x_ref, g_ref, rhi_ref, rlo_ref, x1_ref, hf_ref, gate_ref, afft_ref):
    tm = o_ref.shape[0]
    for r in range(tm // ROW_CHAIN):
        rows = slice(r * ROW_CHAIN, (r + 1) * ROW_CHAIN)
        x1 = x_ref[rows, :] + _dot(o_ref[rows, :], w_ref[...])
        x1_ref[rows, :] = x1
        hf = _rms(x1, g_ref[...])
        hi, lo = _split_bf16(hf)
        hf_ref[rows, :] = hi
        logits = _dot(hi, rhi_ref[...]) + (_dot(hi, rlo_ref[...]) + _dot(lo, rhi_ref[...]))
        lane = lax.broadcasted_iota(jnp.int32, logits.shape, 1)
        logits = jnp.where(lane < N_EXPERTS, logits, NEG_INF)
        m = jnp.max(logits, axis=-1, keepdims=True)
        e = jnp.exp(logits - m)
        aff = e / jnp.sum(e, axis=-1, keepdims=True)
        g_hi, g_lo = _split_bf16(aff)
        gate_ref[rows, :LANES] = g_hi
        gate_ref[rows, LANES:] = g_lo
        afft_ref[:, rows] = aff.T[:N_EXPERTS, :]


def _proj_router(o, w_o, x, g, r_hi, r_lo, tm):
    t = x.shape[0]
    full = lambda shape: pl.BlockSpec(shape, lambda i: (0, 0))
    return pl.pallas_call(
        _proj_router_kernel,
        grid=(t // tm,),
        in_specs=[
            pl.BlockSpec((tm, D_MODEL), lambda i: (i, 0)),
            full((D_MODEL, D_MODEL)),
            pl.BlockSpec((tm, D_MODEL), lambda i: (i, 0)),
            full((1, D_MODEL)),
            full((D_MODEL, LANES)),
            full((D_MODEL, LANES)),
        ],
        out_specs=[
            pl.BlockSpec((tm, D_MODEL), lambda i: (i, 0)),
            pl.BlockSpec((tm, D_MODEL), lambda i: (i, 0)),
            pl.BlockSpec((tm, GATE_LANES), lambda i: (i, 0)),
            pl.BlockSpec((N_EXPERTS, tm), lambda i: (0, i)),
        ],
        out_shape=[
            jax.ShapeDtypeStruct((t, D_MODEL), F32),
            jax.ShapeDtypeStruct((t, D_MODEL), BF16),
            jax.ShapeDtypeStruct((t, GATE_LANES), BF16),
            jax.ShapeDtypeStruct((N_EXPERTS, t), F32),
        ],
        compiler_params=_params(1),
        name="proj_router",
    )(o, w_o, x, g, r_hi, r_lo)


def _select_kernel(aff_ref, upper_ref, lower_ref, pos_ref, offs_ref, *, cap):
    aff = aff_ref[...]
    ne, nr, _ = aff.shape

    def count(mask):
        return jnp.sum(jnp.sum(jnp.where(mask, 1.0, 0.0), axis=2, keepdims=True), axis=1, keepdims=True)

    def bisect(i, bits):
        cand = bits | jnp.left_shift(jnp.int32(1), 30 - i)
        return jnp.where(count(aff >= pltpu.bitcast(cand, F32)) >= cap, cand, bits)

    thr = pltpu.bitcast(lax.fori_loop(0, 31, bisect, jnp.zeros((ne, 1, 1), jnp.int32)), F32)
    above = aff > thr
    tied = aff == thr
    need = cap - count(above)

    ones = jnp.ones((LANES, LANES), BF16)

    def prefix(mask):
        mb = jnp.where(mask, 1.0, 0.0).astype(BF16).reshape(ne * nr, LANES)
        incl = _dot(mb, upper_ref[...]).reshape(ne, nr, LANES)
        tot = _dot(mb, ones).astype(BF16).reshape(ne, nr, LANES)
        offs = jnp.stack([_dot(lower_ref[...], tot[e]) for e in range(ne)], axis=0)
        return offs + incl - mb.astype(F32).reshape(ne, nr, LANES), offs

    tie_rank, _ = prefix(tied)
    sel = above | (tied & (tie_rank < need))
    pos, offs = prefix(sel)
    pos_ref[...] = jnp.where(sel, pos, -1.0).astype(jnp.int32)
    offs_ref[...] = offs.astype(jnp.int32)


def _select(afft, cap):
    ne, n = afft.shape
    nr = n // LANES
    aff3 = afft.reshape(ne, nr, LANES)
    upper = jnp.asarray(np.triu(np.ones((LANES, LANES), np.float32)), BF16)
    lower = jnp.asarray(np.tril(np.ones((nr, nr), np.float32), -1), BF16)
    blk3 = pl.BlockSpec((ne, nr, LANES), lambda i: (0, 0, 0))
    return pl.pallas_call(
        functools.partial(_select_kernel, cap=cap),
        grid=(1,),
        in_specs=[blk3, pl.BlockSpec((LANES, LANES), lambda i: (0, 0)), pl.BlockSpec((nr, nr), lambda i: (0, 0))],
        out_specs=[blk3, blk3],
        out_shape=[jax.ShapeDtypeStruct((ne, nr, LANES), jnp.int32)] * 2,
        compiler_params=_params(1),
        name="ec_select",
    )(aff3, upper, lower)


def _align_down(x):
    return lax.shift_left(lax.shift_right_logical(x, ROW_ALIGN.bit_length() - 1), ROW_ALIGN.bit_length() - 1)


def _window_info(st_ref, e, t, nt):
    start = st_ref[e * (nt + 1) + t]
    end = st_ref[e * (nt + 1) + t + 1]
    base = _align_down(start)
    return base, end, lax.shift_right_logical(end - base, MOE_WIN.bit_length() - 1) + 1


def _dispatch_kernel(st_ref, hf_ref, gate_ref, pos_ref, xe_ref, stage, extra, carry, sems, xsem, *, nt, cap):
    t = pl.program_id(0)
    ne, tt = pos_ref.shape
    slot = t % 2
    hf = jnp.concatenate([hf_ref[...], gate_ref[...]], axis=1)
    rows = lax.broadcasted_iota(jnp.int32, (MOE_WIN, tt), 0)
    info = [_window_info(st_ref, e, t, nt) for e in range(ne)]

    @pl.when(t == 0)
    def _():
        carry[...] = jnp.zeros_like(carry)
        extra[...] = jnp.zeros_like(extra)
        pads = [pltpu.make_async_copy(extra.at[pl.ds(e * MOE_WIN, MOE_WIN), :], xe_ref.at[e, pl.ds(cap, MOE_WIN), :], xsem)
                for e in range(ne)]
        for cp in pads:
            cp.start()
        for cp in pads:
            cp.wait()

    def gathered(k):
        parts = []
        for e in range(ne):
            rel = pos_ref[e:e + 1, :] - (info[e][0] + k * MOE_WIN)
            parts.append(jnp.where(rel == rows, 1.0, 0.0).astype(BF16))
        return _dot(jnp.concatenate(parts, axis=0), hf).astype(BF16)

    def window(buf, e, k, sem):
        row = pl.multiple_of(info[e][0] + k * MOE_WIN, ROW_ALIGN)
        return pltpu.make_async_copy(buf.at[pl.ds(e * MOE_WIN, MOE_WIN), :], xe_ref.at[e, pl.ds(row, MOE_WIN), :], sem)

    stage[slot] = gathered(0)
    for e in range(ne):
        base, end, nwin = info[e]
        head = pl.ds(e * MOE_WIN, ROW_ALIGN)
        stage[slot, head, :] = stage[slot, head, :] + carry[e]

        @pl.when(nwin == 1)
        def _():
            tail = pl.multiple_of(e * MOE_WIN + _align_down(end) - base, ROW_ALIGN)
            carry[e] = stage[slot, pl.ds(tail, ROW_ALIGN), :]

        @pl.when(t > 0)
        def _():
            window(stage.at[1 - slot], e, 0, sems.at[e]).wait()

        window(stage.at[slot], e, 0, sems.at[e]).start()

    def more(k, c):
        extra[...] = gathered(k)
        for e in range(ne):
            base, end, nwin = info[e]

            @pl.when(k < nwin)
            def _():
                cp = window(extra, e, k, xsem)
                cp.start()
                cp.wait()

            @pl.when(k == nwin - 1)
            def _():
                tail = pl.multiple_of(e * MOE_WIN + _align_down(end) - base - k * MOE_WIN, ROW_ALIGN)
                carry[e] = extra[pl.ds(tail, ROW_ALIGN), :]
        return c

    lax.fori_loop(1, functools.reduce(jnp.maximum, [i[2] for i in info]), more, 0)

    @pl.when(t == nt - 1)
    def _():
        for e in range(ne):
            window(stage.at[slot], e, 0, sems.at[e]).wait()


def _dispatch(starts, hf, gates, posm, cap, tt):
    ne, n = posm.shape
    nt = n // tt
    width = D_MODEL + GATE_LANES
    grid_spec = pltpu.PrefetchScalarGridSpec(
        num_scalar_prefetch=1,
        grid=(nt,),
        in_specs=[
            pl.BlockSpec((tt, D_MODEL), lambda t, st: (t, 0)),
            pl.BlockSpec((tt, GATE_LANES), lambda t, st: (t, 0)),
            pl.BlockSpec((ne, tt), lambda t, st: (0, t)),
        ],
        out_specs=pl.BlockSpec(memory_space=pl.ANY),
        scratch_shapes=[
            pltpu.VMEM((2, ne * MOE_WIN, width), BF16),
            pltpu.VMEM((ne * MOE_WIN, width), BF16),
            pltpu.VMEM((ne, ROW_ALIGN, width), BF16),
            pltpu.SemaphoreType.DMA((ne,)),
            pltpu.SemaphoreType.DMA(()),
        ],
    )
    return pl.pallas_call(
        functools.partial(_dispatch_kernel, nt=nt, cap=cap),
        grid_spec=grid_spec,
        out_shape=jax.ShapeDtypeStruct((ne, cap + MOE_WIN, width), BF16),
        compiler_params=_params(1),
        name="ec_dispatch",
    )(starts, hf, gates, posm)


def _combine_kernel(st_ref, y_ref, posc_ref, x_ref, out_ref, ybuf, extra, sems, xsem, *, nt, cap):
    t = pl.program_id(0)
    tt, ne = posc_ref.shape
    slot = t % 2
    last = cap - MOE_WIN

    def main_row(tile, e):
        return jnp.minimum(_window_info(st_ref, e, tile, nt)[0], last)

    def window(row, e, buf, sem):
        src = y_ref.at[e, pl.ds(pl.multiple_of(row, ROW_ALIGN), MOE_WIN), :]
        return pltpu.make_async_copy(src, buf.at[pl.ds(e * MOE_WIN, MOE_WIN), :], sem)

    @pl.when(t == 0)
    def _():
        extra[...] = jnp.zeros_like(extra)
        for e in range(ne):
            window(main_row(0, e), e, ybuf.at[0], sems.at[0, e]).start()

    @pl.when(t + 1 < nt)
    def _():
        for e in range(ne):
            window(main_row(t + 1, e), e, ybuf.at[1 - slot], sems.at[1 - slot, e]).start()

    row0 = [main_row(t, e) for e in range(ne)]
    for e in range(ne):
        window(row0[e], e, ybuf.at[slot], sems.at[slot, e]).wait()

    def onehot(rows, floors, toks=slice(None)):
        parts = []
        for e in range(ne):
            pos = posc_ref[toks, e:e + 1]
            hit = (pos - rows[e]) == lax.broadcasted_iota(jnp.int32, (pos.shape[0], MOE_WIN), 1)
            if floors is not None:
                hit = jnp.logical_and(hit, pos >= floors[e])
            parts.append(jnp.where(hit, 1.0, 0.0).astype(BF16))
        return jnp.concatenate(parts, axis=1)

    y = ybuf[slot]
    for r in range(tt // ROW_CHAIN):
        rows = slice(r * ROW_CHAIN, (r + 1) * ROW_CHAIN)
        out_ref[rows, :] = x_ref[rows, :] + _dot(onehot(row0, None, rows), y)

    ends = [_window_info(st_ref, e, t, nt)[1] for e in range(ne)]
    nwin = [lax.shift_right_logical(jnp.maximum(ends[e] - row0[e], 1) - 1, MOE_WIN.bit_length() - 1) + 1 for e in range(ne)]

    def more(k, c):
        floors = [row0[e] + k * MOE_WIN for e in range(ne)]
        rows = [jnp.minimum(f, last) for f in floors]
        for e in range(ne):
            @pl.when(k < nwin[e])
            def _():
                cp = window(rows[e], e, extra, xsem)
                cp.start()
                cp.wait()
        out_ref[...] += _dot(onehot(rows, floors), extra[...])
        return c

    lax.fori_loop(1, functools.reduce(jnp.maximum, nwin), more, 0)


def _combine(starts, y, posc, x, tt):
    n, ne = posc.shape
    nt = n // tt
    grid_spec = pltpu.PrefetchScalarGridSpec(
        num_scalar_prefetch=1,
        grid=(nt,),
        in_specs=[
            pl.BlockSpec(memory_space=pl.ANY),
            pl.BlockSpec((tt, ne), lambda t, st: (t, 0)),
            pl.BlockSpec((tt, D_MODEL), lambda t, st: (t, 0)),
        ],
        out_specs=pl.BlockSpec((tt, D_MODEL), lambda t, st: (t, 0)),
        scratch_shapes=[
            pltpu.VMEM((2, ne * MOE_WIN, D_MODEL), BF16),
            pltpu.VMEM((ne * MOE_WIN, D_MODEL), BF16),
            pltpu.SemaphoreType.DMA((2, ne)),
            pltpu.SemaphoreType.DMA(()),
        ],
    )
    return pl.pallas_call(
        functools.partial(_combine_kernel, nt=nt, cap=y.shape[1]),
        grid_spec=grid_spec,
        out_shape=jax.ShapeDtypeStruct((n, D_MODEL), F32),
        compiler_params=_params(1),
        name="ec_combine",
    )(starts, y, posc, x)


def _ffn_kernel(x_ref, wg_ref, wu_ref, wd_ref, y_ref, *, fc):
    x = x_ref[0, :, :D_MODEL]
    parts = x_ref[0, :, D_MODEL:].astype(F32)
    lane = lax.broadcasted_iota(jnp.int32, parts.shape, 1)
    gate = jnp.sum(jnp.where((lane % LANES) == pl.program_id(0), parts, 0.0), axis=1, keepdims=True)
    acc = jnp.zeros(y_ref.shape[1:], F32)
    for c in range(EXPERT_FF // fc):
        sl = slice(c * fc, (c + 1) * fc)
        a = _dot(x, wg_ref[0, :, sl])
        u = _dot(x, wu_ref[0, :, sl])
        hid = (a * (1.0 / (1.0 + jnp.exp(-a))) * u).astype(BF16)
        acc = acc + _dot(hid, wd_ref[0, sl, :])
    y_ref[0] = (acc * gate).astype(BF16)


def _ffn(xe, wg, wu, wd, cap, bm, fc):
    ne = wg.shape[0]
    return pl.pallas_call(
        functools.partial(_ffn_kernel, fc=fc),
        grid=(ne, cap // bm),
        in_specs=[
            pl.BlockSpec((1, bm, D_MODEL + GATE_LANES), lambda e, j: (e, j, 0)),
            pl.BlockSpec((1, D_MODEL, EXPERT_FF), lambda e, j: (e, 0, 0)),
            pl.BlockSpec((1, D_MODEL, EXPERT_FF), lambda e, j: (e, 0, 0)),
            pl.BlockSpec((1, EXPERT_FF, D_MODEL), lambda e, j: (e, 0, 0)),
        ],
        out_specs=pl.BlockSpec((1, bm, D_MODEL), lambda e, j: (e, j, 0)),
        out_shape=jax.ShapeDtypeStruct((ne, cap, D_MODEL), BF16),
        compiler_params=_params(2),
        name="ec_ffn",
    )(xe, wg, wu, wd)


def _tile(n, pref):
    return pref if n % pref == 0 else n


def _rope_tables(n):
    tok = jnp.arange(n, dtype=jnp.int32)
    pos = jnp.stack([tok // GRID_W, tok % GRID_W], axis=-1).astype(F32)
    inv = ROPE_THETA ** (-jnp.arange(0, ROPE_AXIS_DIM, 2, dtype=F32) / ROPE_AXIS_DIM)
    ang = pos[:, :, None] * inv
    cos, sin = jnp.cos(ang), jnp.sin(ang)
    half = ROPE_AXIS_DIM // 2
    cos_h = jnp.concatenate([cos, cos], axis=-1).reshape(n, HEAD_DIM)
    zeros = jnp.zeros_like(sin)
    s1_h = jnp.concatenate([zeros, sin], axis=-1).reshape(n, HEAD_DIM)
    s2_h = jnp.concatenate([-sin, zeros], axis=-1).reshape(n, HEAD_DIM)
    del half
    rep = lambda a: jnp.concatenate([a, a], axis=-1)
    return rep(cos_h), rep(s1_h), rep(s2_h)


def _block_diag_ones():
    idx = np.arange(MXU_DIM) // HEAD_DIM
    return jnp.asarray((idx[:, None] == idx[None, :]).astype(np.float32), BF16)


_A_GROUP = N_HEADS // A_KV_HEADS
_Q_PERM = np.array([kvp * 2 * _A_GROUP + half * _A_GROUP + g
                    for kvp in range(A_KV_HEADS // 2) for g in range(_A_GROUP) for half in range(2)])


def _perm_head_cols(w):
    d = w.shape[0]
    return w.reshape(d, N_HEADS, HEAD_DIM)[:, _Q_PERM].reshape(d, N_HEADS * HEAD_DIM)


def _moe(x1, hf, gates, afft, wg, wu, wd):
    n = x1.shape[0]
    cap = max(1, CAPACITY_FACTOR * n // N_EXPERTS)
    tt = _tile(n, MOE_TILE)
    bm = _tile(cap, FFN_ROWS)
    pos3, offs3 = _select(afft, cap)
    starts = jnp.concatenate([offs3[:, ::tt // LANES, 0], jnp.full((N_EXPERTS, 1), cap, jnp.int32)], axis=1)
    starts = starts.reshape(-1)
    posm = pos3.reshape(N_EXPERTS, n)
    xe = _dispatch(starts, hf, gates, posm, cap, tt)
    y = _ffn(xe, wg, wu, wd, cap, bm, 512)
    return _combine(starts, y, posm.T, x1, tt)


def _trunk(x, p):
    b, n, _ = x.shape
    t = b * n
    tm = _tile(n, 512)
    x2 = x.reshape(t, D_MODEL)
    for i in range(p["depth"]):
        j = i // 2
        g_mix = p["norm_mix"][i].reshape(1, D_MODEL)
        if i % 2 == 0:
            q, k, v = _gqa_qkv(x2, g_mix, p["a_w_qkv"][j], p["bd"], p["a_gains"][j], *p["rope"][n], n, tm)
            o = _gqa_attn(p["a_bound"][j], q, k, v, b, n, _tile(n, 256), _tile(n, 512))
            w_o = p["a_w_o"][j]
        else:
            q, k, v = _na_qkv(x2, g_mix, p["b_w_qkv"][j], p["bd"], p["b_gains"][j], tm)
            o = _na_attn(p["b_bound"][j], q, k, v, p["b_bias"][j], b, n, 4)
            w_o = p["b_w_o"][j]
        x1, hf, gates, afft = _proj_router(o, w_o, x2, p["norm_ffn"][i].reshape(1, D_MODEL),
                                          p["r_hi"][i], p["r_lo"][i], tm)
        x2 = _moe(x1, hf, gates, afft, p["wg"][i], p["wu"][i], p["wd"][i])
    return x2.reshape(b, n, D_MODEL)


def kernel(x_prompt, x_sample, norm_mix, norm_ffn, a_w_qkv, a_q_norm, a_k_norm, a_w_o, b_w_qkv, b_q_norm, b_k_norm,
           b_rpb, b_w_o, moe_router, moe_w_gate, moe_w_up, moe_w_down):
    depth = norm_mix.shape[0]
    dq = N_HEADS * HEAD_DIM
    n_a, n_b = a_w_qkv.shape[0], b_w_qkv.shape[0]
    a_w = jnp.concatenate([jnp.stack([_perm_head_cols(a_w_qkv[j, :, :dq]) for j in range(n_a)]),
                           a_w_qkv[:, :, dq:]], axis=-1).astype(BF16)
    a_wo = jnp.stack([a_w_o[j].reshape(N_HEADS, HEAD_DIM, D_MODEL)[_Q_PERM].reshape(dq, D_MODEL)
                      for j in range(n_a)]).astype(BF16)
    a_gains = jnp.concatenate([jnp.tile(a_q_norm, (1, N_HEADS)) * Q_GAIN_SCALE,
                               jnp.tile(a_k_norm, (1, A_KV_HEADS))], axis=-1).reshape(n_a, 1, -1)
    b_gains = jnp.concatenate([jnp.tile(b_q_norm, (1, N_HEADS)) * Q_GAIN_SCALE,
                               jnp.tile(b_k_norm, (1, N_HEADS))], axis=-1).reshape(n_b, 1, -1)
    r_pad = jnp.pad(moe_router, ((0, 0), (0, 0), (0, LANES - N_EXPERTS)))
    r_hi = r_pad.astype(BF16)
    r_lo = (r_pad - r_hi.astype(F32)).astype(BF16)
    a_bound = (HEAD_DIM * 1.01) * (jnp.max(jnp.abs(a_gains[:, 0, :dq]), axis=-1, keepdims=True)
                                   * jnp.max(jnp.abs(a_gains[:, 0, dq:]), axis=-1, keepdims=True))
    b_bound = (HEAD_DIM * 1.01) * (jnp.max(jnp.abs(b_gains[:, 0, :dq]), axis=-1, keepdims=True)
                                   * jnp.max(jnp.abs(b_gains[:, 0, dq:]), axis=-1, keepdims=True))
    b_bound = b_bound + LOG2_E * jnp.max(jnp.abs(b_rpb), axis=(1, 2, 3)).reshape(n_b, 1)
    p = {
        "depth": depth, "a_bound": a_bound, "b_bound": b_bound,
        "norm_mix": norm_mix, "norm_ffn": norm_ffn,
        "a_w_qkv": a_w, "a_w_o": a_wo, "a_gains": a_gains,
        "b_w_qkv": b_w_qkv.astype(BF16), "b_w_o": b_w_o.astype(BF16), "b_gains": b_gains,
        "b_bias": jnp.stack([_na_bias_table(b_rpb[j]) for j in range(n_b)]),
        "bd": _block_diag_ones(),
        "r_hi": r_hi, "r_lo": r_lo,
        "wg": moe_w_gate.astype(BF16), "wu": moe_w_up.astype(BF16), "wd": moe_w_down.astype(BF16),
        "rope": {n: _rope_tables(n) for n in {x_prompt.shape[1], x_sample.shape[1]}},
    }
    return _trunk(x_prompt, p), _trunk(x_sample, p)
```

```python
import functools

import jax
import jax.numpy as jnp
import numpy as np
from jax import lax
from jax.experimental import pallas as pl
from jax.experimental.pallas import tpu as pltpu

D_MODEL = 1024
HEAD_DIM = 64
N_HEADS = 16
A_KV_HEADS = 4
GRID_W = 64
ROPE_THETA = 10000.0
ROPE_AXIS_DIM = HEAD_DIM // 2
NA_ROWS = 8
NA_KW = 16
NA_SUB = 8
NA_QB = 4
N_EXPERTS = 16
CAPACITY_FACTOR = 2
EXPERT_FF = 2048
RMS_EPS = 1e-6
NEG_INF = -1e30
ATTN_SCALE = HEAD_DIM ** -0.5
LOG2_E = float(np.log2(np.e))
Q_GAIN_SCALE = ATTN_SCALE * LOG2_E
SOFTMAX_SHIFT_LIMIT = 56.0
GQA_UNROLL = 8

LANES = 128
MXU_DIM = 256
VMEM_LIMIT = 48 * 1024 * 1024
ROW_ALIGN = 16
MOE_WIN = 128
MOE_TILE = 512
GATE_LANES = 2 * LANES
ROW_CHAIN = 256

BF16 = jnp.bfloat16
F32 = jnp.float32


def _params(n_axes, vmem=VMEM_LIMIT):
    return pltpu.CompilerParams(dimension_semantics=("arbitrary",) * n_axes, vmem_limit_bytes=vmem)


def _dot(a, b):
    return jnp.dot(a, b, preferred_element_type=F32)


def _dot_nt(a, b):
    return lax.dot_general(a, b, (((1,), (1,)), ((), ())), preferred_element_type=F32)


def _rms(x, g):
    ms = jnp.mean(x * x, axis=-1, keepdims=True)
    return x * lax.rsqrt(ms + RMS_EPS) * g


def _split_bf16(x):
    hi = x.astype(BF16)
    lo = (x - hi.astype(F32)).astype(BF16)
    return hi, lo


def _head_norm(y, bd, gains):
    sq = y * y
    hi, lo = _split_bf16(sq)
    w = y.shape[1]
    parts = []
    for c in range(w // MXU_DIM):
        sl = slice(c * MXU_DIM, (c + 1) * MXU_DIM)
        parts.append(_dot(hi[:, sl], bd) + _dot(lo[:, sl], bd))
    ss = jnp.concatenate(parts, axis=1)
    return y * lax.rsqrt(ss * (1.0 / HEAD_DIM) + RMS_EPS) * gains


def _gqa_qkv_kernel(x_ref, g_ref, w_ref, bd_ref, gains_ref, cos_ref, s1_ref, s2_ref, q_ref, k_ref, v_ref):
    dq = N_HEADS * HEAD_DIM
    dkv = A_KV_HEADS * HEAD_DIM
    half = ROPE_AXIS_DIM // 2
    for r in range(x_ref.shape[0] // ROW_CHAIN):
        rows = slice(r * ROW_CHAIN, (r + 1) * ROW_CHAIN)
        h = _rms(x_ref[rows, :], g_ref[...]).astype(BF16)
        qkv = _dot(h, w_ref[...])
        qk = _head_norm(qkv[:, :dq + dkv], bd_ref[...], gains_ref[...])
        cos, s1, s2 = cos_ref[rows, :], s1_ref[rows, :], s2_ref[rows, :]
        outs = []
        for c in range((dq + dkv) // LANES):
            xc = qk[:, c * LANES:(c + 1) * LANES]
            outs.append(xc * cos + pltpu.roll(xc, half, 1) * s1 + pltpu.roll(xc, LANES - half, 1) * s2)
        roped = jnp.concatenate(outs, axis=1)
        q_ref[rows, :] = roped[:, :dq].astype(BF16)
        k_ref[rows, :] = roped[:, dq:].astype(BF16)
        v_ref[rows, :] = qkv[:, dq + dkv:].astype(BF16)


def _gqa_qkv(x, g, w, bd, gains, cos, s1, s2, n, tm):
    t = x.shape[0]
    dq = N_HEADS * HEAD_DIM
    dkv = A_KV_HEADS * HEAD_DIM
    npos = n // tm
    full = lambda shape: pl.BlockSpec(shape, lambda i: (0, 0))
    return pl.pallas_call(
        _gqa_qkv_kernel,
        grid=(t // tm,),
        in_specs=[
            pl.BlockSpec((tm, D_MODEL), lambda i: (i, 0)),
            full((1, D_MODEL)),
            full((D_MODEL, dq + 2 * dkv)),
            full((MXU_DIM, MXU_DIM)),
            full((1, dq + dkv)),
            pl.BlockSpec((tm, LANES), lambda i: (i % npos, 0)),
            pl.BlockSpec((tm, LANES), lambda i: (i % npos, 0)),
            pl.BlockSpec((tm, LANES), lambda i: (i % npos, 0)),
        ],
        out_specs=[
            pl.BlockSpec((tm, dq), lambda i: (i, 0)),
            pl.BlockSpec((tm, dkv), lambda i: (i, 0)),
            pl.BlockSpec((tm, dkv), lambda i: (i, 0)),
        ],
        out_shape=[
            jax.ShapeDtypeStruct((t, dq), BF16),
            jax.ShapeDtypeStruct((t, dkv), BF16),
            jax.ShapeDtypeStruct((t, dkv), BF16),
        ],
        compiler_params=_params(1),
        name="gqa_qkv",
    )(x, g, w, bd, gains, cos, s1, s2)


def _gqa_attn_kernel(bound_ref, q_ref, k_ref, v_ref, o_ref, *, n, tk, unroll):
    tq = q_ref.shape[0]
    lane = lax.broadcasted_iota(jnp.int32, (tq, LANES), 1)
    low = lane < HEAD_DIM
    bound = bound_ref[0]

    def key_tile(j):
        start = pl.multiple_of(j * tk, tk)
        return k_ref[pl.ds(start, tk), :], v_ref[pl.ds(start, tk), :]

    for c in range(q_ref.shape[1] // LANES):
        cols = slice(c * LANES, (c + 1) * LANES)
        qc = q_ref[:, cols]
        zero = jnp.zeros_like(qc)
        qa = jnp.where(low, qc, zero)
        qb = jnp.where(low, zero, qc)

        @pl.when(bound <= SOFTMAX_SHIFT_LIMIT)
        def _():
            def body(j, carry):
                ls_a, ls_b, acc = carry
                kt, vt = key_tile(j)
                p_a = jnp.exp2(_dot_nt(qa, kt) - bound)
                p_b = jnp.exp2(_dot_nt(qb, kt) - bound)
                for i in range(tk // LANES):
                    ls_a = ls_a + p_a[:, i * LANES:(i + 1) * LANES]
                    ls_b = ls_b + p_b[:, i * LANES:(i + 1) * LANES]
                pv_a = _dot(p_a.astype(BF16), vt)
                pv_b = _dot(p_b.astype(BF16), vt)
                return ls_a, ls_b, acc + jnp.where(low, pv_a, pv_b)

            z = jnp.zeros((tq, LANES), F32)
            ls_a, ls_b, acc = lax.fori_loop(0, n // tk, body, (z, z, z), unroll=unroll)
            l_a = jnp.sum(ls_a, axis=1, keepdims=True)
            l_b = jnp.sum(ls_b, axis=1, keepdims=True)
            o_ref[:, cols] = (acc / jnp.where(low, l_a, l_b)).astype(BF16)

        @pl.when(bound > SOFTMAX_SHIFT_LIMIT)
        def _():
            def body(j, carry):
                m_a, l_a, m_b, l_b, acc = carry
                kt, vt = key_tile(j)
                s_a = _dot_nt(qa, kt)
                s_b = _dot_nt(qb, kt)
                mn_a = jnp.maximum(m_a, jnp.max(s_a, axis=-1, keepdims=True))
                mn_b = jnp.maximum(m_b, jnp.max(s_b, axis=-1, keepdims=True))
                p_a = jnp.exp2(s_a - mn_a)
                p_b = jnp.exp2(s_b - mn_b)
                al_a = jnp.exp2(m_a - mn_a)
                al_b = jnp.exp2(m_b - mn_b)
                l_a = al_a * l_a + jnp.sum(p_a, axis=-1, keepdims=True)
                l_b = al_b * l_b + jnp.sum(p_b, axis=-1, keepdims=True)
                pv_a = _dot(p_a.astype(BF16), vt)
                pv_b = _dot(p_b.astype(BF16), vt)
                acc = acc * jnp.where(low, al_a, al_b) + jnp.where(low, pv_a, pv_b)
                return mn_a, l_a, mn_b, l_b, acc

            neg = jnp.full((tq, 1), NEG_INF, F32)
            zer = jnp.zeros((tq, 1), F32)
            _, l_a, _, l_b, acc = lax.fori_loop(0, n // tk, body, (neg, zer, neg, zer, jnp.zeros((tq, LANES), F32)))
            o_ref[:, cols] = (acc / jnp.where(low, l_a, l_b)).astype(BF16)


def _gqa_attn(bound, q, k, v, b, n, tq, tk):
    t = q.shape[0]
    dq = N_HEADS * HEAD_DIM
    nq = n // tq
    pairs = A_KV_HEADS // 2
    qw = dq // pairs
    return pl.pallas_call(
        functools.partial(_gqa_attn_kernel, n=n, tk=tk, unroll=min(n // tk, GQA_UNROLL)),
        grid=(b, pairs, nq),
        in_specs=[
            pl.BlockSpec(memory_space=pltpu.SMEM),
            pl.BlockSpec((tq, qw), lambda bi, p, i: (bi * nq + i, p)),
            pl.BlockSpec((n, LANES), lambda bi, p, i: (bi, p)),
            pl.BlockSpec((n, LANES), lambda bi, p, i: (bi, p)),
        ],
        out_specs=pl.BlockSpec((tq, qw), lambda bi, p, i: (bi * nq + i, p)),
        out_shape=jax.ShapeDtypeStruct((t, dq), BF16),
        compiler_params=_params(3),
        name="gqa_attn",
    )(bound, q, k, v)


def _na_qkv_kernel(x_ref, g_ref, w_ref, bd_ref, gains_ref, q_ref, k_ref, v_ref):
    dh = N_HEADS * HEAD_DIM
    h = _rms(x_ref[...], g_ref[...]).astype(BF16)
    qkv = _dot(h, w_ref[...])
    qk = _head_norm(qkv[:, :2 * dh], bd_ref[...], gains_ref[...])
    q_ref[...] = qk[:, :dh].astype(BF16)
    k_ref[...] = qk[:, dh:].astype(BF16)
    v_ref[...] = qkv[:, 2 * dh:].astype(BF16)


def _na_qkv(x, g, w, bd, gains, tm):
    t = x.shape[0]
    dh = N_HEADS * HEAD_DIM
    full = lambda shape: pl.BlockSpec(shape, lambda i: (0, 0))
    return pl.pallas_call(
        _na_qkv_kernel,
        grid=(t // tm,),
        in_specs=[
            pl.BlockSpec((tm, D_MODEL), lambda i: (i, 0)),
            full((1, D_MODEL)),
            full((D_MODEL, 3 * dh)),
            full((MXU_DIM, MXU_DIM)),
            full((1, 2 * dh)),
        ],
        out_specs=[pl.BlockSpec((tm, dh), lambda i: (i, 0))] * 3,
        out_shape=[jax.ShapeDtypeStruct((t, dh), BF16)] * 3,
        compiler_params=_params(1),
        name="na_qkv",
    )(x, g, w, bd, gains)


def _na_attn_kernel(bound_ref, q_ref, k_ref, v_ref, bias_ref, o_ref, *, nblk, sub):
    ub = pl.program_id(2)
    blk = NA_QB * GRID_W
    win = 3 * blk
    lane = lax.broadcasted_iota(jnp.int32, (blk, LANES), 1)
    low = lane < HEAD_DIM
    bound = bound_ref[0]

    def block(s, softmax):
        u = ub * sub + s
        var = jnp.where(u == 0, 0, jnp.where(u == nblk - 1, 2, 1))
        start = pl.multiple_of(jnp.clip(u - 1, 0, nblk - 3) * blk, blk)
        kw = k_ref[pl.ds(start, win), :]
        vw = v_ref[pl.ds(start, win), :]
        qs = q_ref[s * blk:(s + 1) * blk, :]
        zero = jnp.zeros_like(qs)
        halves = []
        for hh, qm in enumerate((jnp.where(low, qs, zero), jnp.where(low, zero, qs))):
            p, l = softmax(_dot_nt(qm, kw) + bias_ref[var, hh])
            halves.append(_dot(p.astype(BF16), vw) / l)
        o_ref[s * blk:(s + 1) * blk, :] = jnp.where(low, halves[0], halves[1]).astype(BF16)

    def shifted(sc):
        p = jnp.exp2(sc - bound)
        ls = p[:, :LANES]
        for i in range(1, win // LANES):
            ls = ls + p[:, i * LANES:(i + 1) * LANES]
        return p, jnp.sum(ls, axis=1, keepdims=True)

    def running_max(sc):
        p = jnp.exp2(sc - jnp.max(sc, axis=-1, keepdims=True))
        return p, jnp.sum(p, axis=-1, keepdims=True)

    @pl.when(bound <= SOFTMAX_SHIFT_LIMIT)
    def _():
        for s in range(sub):
            block(s, shifted)

    @pl.when(bound > SOFTMAX_SHIFT_LIMIT)
    def _():
        for s in range(sub):
            block(s, running_max)


def _na_attn(bound, q, k, v, bias, b, n, sub):
    t = q.shape[0]
    dh = N_HEADS * HEAD_DIM
    blk = NA_QB * GRID_W
    nblk = n // blk
    assert nblk >= 3 and nblk % sub == 0
    nstep = nblk // sub
    tq = sub * blk
    return pl.pallas_call(
        functools.partial(_na_attn_kernel, nblk=nblk, sub=sub),
        grid=(N_HEADS // 2, b, nstep),
        in_specs=[
            pl.BlockSpec(memory_space=pltpu.SMEM),
            pl.BlockSpec((tq, LANES), lambda p, bi, i: (bi * nstep + i, p)),
            pl.BlockSpec((n, LANES), lambda p, bi, i: (bi, p)),
            pl.BlockSpec((n, LANES), lambda p, bi, i: (bi, p)),
            pl.BlockSpec((3, 2, blk, 3 * blk), lambda p, bi, i: (0, p, 0, 0)),
        ],
        out_specs=pl.BlockSpec((tq, LANES), lambda p, bi, i: (bi * nstep + i, p)),
        out_shape=jax.ShapeDtypeStruct((t, dh), BF16),
        compiler_params=_params(3),
        name="na_attn",
    )(bound, q, k, v, bias)


def _na_bias_table(rpb):
    nkr = 3 * NA_QB
    i = np.arange(NA_QB)[:, None]
    kr = np.arange(nkr)[None, :]
    variants = [(0 * i, NA_ROWS - 1), (i, NA_ROWS - 1 - NA_QB), (NA_QB + 0 * i, NA_ROWS - 1 - 2 * NA_QB)]
    row_valid = np.stack([(kr >= r0) & (kr < r0 + NA_ROWS) for r0, _ in variants])
    dr = np.stack([np.clip(kr - i + off, 0, 2 * NA_ROWS - 2) for _, off in variants])
    row_sel = (dr[..., None] == np.arange(2 * NA_ROWS - 1)).astype(np.float32)
    qc = np.arange(GRID_W)[:, None]
    kc = np.arange(GRID_W)[None, :]
    win0 = np.clip(qc - NA_KW // 2, 0, GRID_W - NA_KW)
    col_valid = (kc >= win0) & (kc < win0 + NA_KW)
    dc = np.clip(kc - qc + NA_KW - 1, 0, 2 * NA_KW - 2)
    col_sel = (dc[..., None] == np.arange(2 * NA_KW - 1)).astype(np.float32)
    hp = lax.Precision.HIGHEST
    tmp = jnp.einsum("vikd,hde->vhike", row_sel, rpb.astype(F32), precision=hp)
    tbl = jnp.einsum("vhike,qce->vhiqkc", tmp, col_sel, precision=hp)
    valid = row_valid[:, :, None, :, None] & col_valid[None, None, :, None, :]
    tbl = jnp.where(valid[:, None], tbl * LOG2_E, NEG_INF)
    return tbl.reshape(3, N_HEADS, NA_QB * GRID_W, nkr * GRID_W)


def _proj_router_kernel(o_ref, w_ref, x_ref, g_ref, rcat_ref, x1_ref, hf_ref, gate_ref, afft_ref):
    tm = o_ref.shape[0]
    for r in range(tm // ROW_CHAIN):
        rows = slice(r * ROW_CHAIN, (r + 1) * ROW_CHAIN)
        x1 = x_ref[rows, :] + _dot(o_ref[rows, :], w_ref[...])
        x1_ref[rows, :] = x1
        hf = _rms(x1, g_ref[...])
        hi, lo = _split_bf16(hf)
        hf_ref[rows, :] = hi
        both = _dot(hi, rcat_ref[...])
        logits = both[:, :LANES] + (both[:, LANES:] + _dot(lo, rcat_ref[:, :LANES]))
        lane = lax.broadcasted_iota(jnp.int32, logits.shape, 1)
        logits = jnp.where(lane < N_EXPERTS, logits, NEG_INF)
        m = jnp.max(logits, axis=-1, keepdims=True)
        e = jnp.exp(logits - m)
        aff = e / jnp.sum(e, axis=-1, keepdims=True)
        g_hi, g_lo = _split_bf16(aff)
        gate_ref[rows, :LANES] = g_hi
        gate_ref[rows, LANES:] = g_lo
        afft_ref[:, rows] = aff.T[:N_EXPERTS, :]


def _proj_router(o, w_o, x, g, r_cat, tm):
    t = x.shape[0]
    full = lambda shape: pl.BlockSpec(shape, lambda i: (0, 0))
    return pl.pallas_call(
        _proj_router_kernel,
        grid=(t // tm,),
        in_specs=[
            pl.BlockSpec((tm, D_MODEL), lambda i: (i, 0)),
            full((D_MODEL, D_MODEL)),
            pl.BlockSpec((tm, D_MODEL), lambda i: (i, 0)),
            full((1, D_MODEL)),
            full((D_MODEL, 2 * LANES)),
        ],
        out_specs=[
            pl.BlockSpec((tm, D_MODEL), lambda i: (i, 0)),
            pl.BlockSpec((tm, D_MODEL), lambda i: (i, 0)),
            pl.BlockSpec((tm, GATE_LANES), lambda i: (i, 0)),
            pl.BlockSpec((N_EXPERTS, tm), lambda i: (0, i)),
        ],
        out_shape=[
            jax.ShapeDtypeStruct((t, D_MODEL), F32),
            jax.ShapeDtypeStruct((t, D_MODEL), BF16),
            jax.ShapeDtypeStruct((t, GATE_LANES), BF16),
            jax.ShapeDtypeStruct((N_EXPERTS, t), F32),
        ],
        compiler_params=_params(1),
        name="proj_router",
    )(o, w_o, x, g, r_cat)


def _select_kernel(aff_ref, upper_ref, lower_ref, pos_ref, offs_ref, *, cap):
    aff = aff_ref[...]
    ne, nr, _ = aff.shape

    def count(mask):
        return jnp.sum(jnp.sum(jnp.where(mask, 1.0, 0.0), axis=2, keepdims=True), axis=1, keepdims=True)

    def bisect(i, bits):
        cand = bits | jnp.left_shift(jnp.int32(1), 30 - i)
        return jnp.where(count(aff >= pltpu.bitcast(cand, F32)) >= cap, cand, bits)

    thr = pltpu.bitcast(lax.fori_loop(0, 31, bisect, jnp.zeros((ne, 1, 1), jnp.int32)), F32)
    above = aff > thr
    tied = aff == thr
    need = cap - count(above)

    ones = jnp.ones((LANES, LANES), BF16)

    def prefix(mask):
        mb = jnp.where(mask, 1.0, 0.0).astype(BF16).reshape(ne * nr, LANES)
        incl = _dot(mb, upper_ref[...]).reshape(ne, nr, LANES)
        tot = _dot(mb, ones).astype(BF16).reshape(ne, nr, LANES)
        offs = jnp.stack([_dot(lower_ref[...], tot[e]) for e in range(ne)], axis=0)
        return offs + incl - mb.astype(F32).reshape(ne, nr, LANES), offs

    tie_rank, _ = prefix(tied)
    sel = above | (tied & (tie_rank < need))
    pos, offs = prefix(sel)
    pos_ref[...] = jnp.where(sel, pos, -1.0).astype(jnp.int32)
    offs_ref[...] = offs.astype(jnp.int32)


def _select(afft, cap):
    ne, n = afft.shape
    nr = n // LANES
    aff3 = afft.reshape(ne, nr, LANES)
    upper = jnp.asarray(np.triu(np.ones((LANES, LANES), np.float32)), BF16)
    lower = jnp.asarray(np.tril(np.ones((nr, nr), np.float32), -1), BF16)
    blk3 = pl.BlockSpec((ne, nr, LANES), lambda i: (0, 0, 0))
    return pl.pallas_call(
        functools.partial(_select_kernel, cap=cap),
        grid=(1,),
        in_specs=[blk3, pl.BlockSpec((LANES, LANES), lambda i: (0, 0)), pl.BlockSpec((nr, nr), lambda i: (0, 0))],
        out_specs=[blk3, blk3],
        out_shape=[jax.ShapeDtypeStruct((ne, nr, LANES), jnp.int32)] * 2,
        compiler_params=_params(1),
        name="ec_select",
    )(aff3, upper, lower)


def _align_down(x):
    return lax.shift_left(lax.shift_right_logical(x, ROW_ALIGN.bit_length() - 1), ROW_ALIGN.bit_length() - 1)


def _window_info(st_ref, e, t, nt):
    start = st_ref[e * (nt + 1) + t]
    end = st_ref[e * (nt + 1) + t + 1]
    base = _align_down(start)
    return base, end, lax.shift_right_logical(end - base, MOE_WIN.bit_length() - 1) + 1


def _dispatch_kernel(st_ref, hf_ref, gate_ref, pos_ref, xe_ref, stage, extra, carry, sems, xsem, *, nt, cap):
    t = pl.program_id(0)
    ne, tt = pos_ref.shape
    slot = t % 2
    hf = jnp.concatenate([hf_ref[...], gate_ref[...]], axis=1)
    rows = lax.broadcasted_iota(jnp.int32, (MOE_WIN, tt), 0)
    info = [_window_info(st_ref, e, t, nt) for e in range(ne)]

    @pl.when(t == 0)
    def _():
        carry[...] = jnp.zeros_like(carry)
        extra[...] = jnp.zeros_like(extra)
        pads = [pltpu.make_async_copy(extra.at[pl.ds(e * MOE_WIN, MOE_WIN), :], xe_ref.at[e, pl.ds(cap, MOE_WIN), :], xsem)
                for e in range(ne)]
        for cp in pads:
            cp.start()
        for cp in pads:
            cp.wait()

    def gathered(k):
        parts = []
        for e in range(ne):
            rel = pos_ref[e:e + 1, :] - (info[e][0] + k * MOE_WIN)
            parts.append(jnp.where(rel == rows, 1.0, 0.0).astype(BF16))
        return _dot(jnp.concatenate(parts, axis=0), hf).astype(BF16)

    def window(buf, e, k, sem):
        row = pl.multiple_of(info[e][0] + k * MOE_WIN, ROW_ALIGN)
        return pltpu.make_async_copy(buf.at[pl.ds(e * MOE_WIN, MOE_WIN), :], xe_ref.at[e, pl.ds(row, MOE_WIN), :], sem)

    stage[slot] = gathered(0)
    for e in range(ne):
        base, end, nwin = info[e]
        head = pl.ds(e * MOE_WIN, ROW_ALIGN)
        stage[slot, head, :] = stage[slot, head, :] + carry[e]

        @pl.when(nwin == 1)
        def _():
            tail = pl.multiple_of(e * MOE_WIN + _align_down(end) - base, ROW_ALIGN)
            carry[e] = stage[slot, pl.ds(tail, ROW_ALIGN), :]

        @pl.when(t > 0)
        def _():
            window(stage.at[1 - slot], e, 0, sems.at[e]).wait()

        window(stage.at[slot], e, 0, sems.at[e]).start()

    def more(k, c):
        extra[...] = gathered(k)
        for e in range(ne):
            base, end, nwin = info[e]

            @pl.when(k < nwin)
            def _():
                cp = window(extra, e, k, xsem)
                cp.start()
                cp.wait()

            @pl.when(k == nwin - 1)
            def _():
                tail = pl.multiple_of(e * MOE_WIN + _align_down(end) - base - k * MOE_WIN, ROW_ALIGN)
                carry[e] = extra[pl.ds(tail, ROW_ALIGN), :]
        return c

    lax.fori_loop(1, functools.reduce(jnp.maximum, [i[2] for i in info]), more, 0)

    @pl.when(t == nt - 1)
    def _():
        for e in range(ne):
            window(stage.at[slot], e, 0, sems.at[e]).wait()


def _dispatch(starts, hf, gates, posm, cap, tt):
    ne, n = posm.shape
    nt = n // tt
    width = D_MODEL + GATE_LANES
    grid_spec = pltpu.PrefetchScalarGridSpec(
        num_scalar_prefetch=1,
        grid=(nt,),
        in_specs=[
            pl.BlockSpec((tt, D_MODEL), lambda t, st: (t, 0)),
            pl.BlockSpec((tt, GATE_LANES), lambda t, st: (t, 0)),
            pl.BlockSpec((ne, tt), lambda t, st: (0, t)),
        ],
        out_specs=pl.BlockSpec(memory_space=pl.ANY),
        scratch_shapes=[
            pltpu.VMEM((2, ne * MOE_WIN, width), BF16),
            pltpu.VMEM((ne * MOE_WIN, width), BF16),
            pltpu.VMEM((ne, ROW_ALIGN, width), BF16),
            pltpu.SemaphoreType.DMA((ne,)),
            pltpu.SemaphoreType.DMA(()),
        ],
    )
    return pl.pallas_call(
        functools.partial(_dispatch_kernel, nt=nt, cap=cap),
        grid_spec=grid_spec,
        out_shape=jax.ShapeDtypeStruct((ne, cap + MOE_WIN, width), BF16),
        compiler_params=_params(1),
        name="ec_dispatch",
    )(starts, hf, gates, posm)


def _combine_kernel(st_ref, y_ref, posc_ref, x_ref, out_ref, ybuf, extra, sems, xsem, *, nt, cap):
    t = pl.program_id(0)
    tt, ne = posc_ref.shape
    slot = t % 2
    last = cap - MOE_WIN

    def main_row(tile, e):
        return jnp.minimum(_window_info(st_ref, e, tile, nt)[0], last)

    def window(row, e, buf, sem):
        src = y_ref.at[e, pl.ds(pl.multiple_of(row, ROW_ALIGN), MOE_WIN), :]
        return pltpu.make_async_copy(src, buf.at[pl.ds(e * MOE_WIN, MOE_WIN), :], sem)

    @pl.when(t == 0)
    def _():
        extra[...] = jnp.zeros_like(extra)
        for e in range(ne):
            window(main_row(0, e), e, ybuf.at[0], sems.at[0, e]).start()

    @pl.when(t + 1 < nt)
    def _():
        for e in range(ne):
            window(main_row(t + 1, e), e, ybuf.at[1 - slot], sems.at[1 - slot, e]).start()

    row0 = [main_row(t, e) for e in range(ne)]
    for e in range(ne):
        window(row0[e], e, ybuf.at[slot], sems.at[slot, e]).wait()

    def onehot(rows, floors, toks=slice(None)):
        parts = []
        for e in range(ne):
            pos = posc_ref[toks, e:e + 1]
            hit = (pos - rows[e]) == lax.broadcasted_iota(jnp.int32, (pos.shape[0], MOE_WIN), 1)
            if floors is not None:
                hit = jnp.logical_and(hit, pos >= floors[e])
            parts.append(jnp.where(hit, 1.0, 0.0).astype(BF16))
        return jnp.concatenate(parts, axis=1)

    y = ybuf[slot]
    for r in range(tt // ROW_CHAIN):
        rows = slice(r * ROW_CHAIN, (r + 1) * ROW_CHAIN)
        out_ref[rows, :] = x_ref[rows, :] + _dot(onehot(row0, None, rows), y)

    ends = [_window_info(st_ref, e, t, nt)[1] for e in range(ne)]
    nwin = [lax.shift_right_logical(jnp.maximum(ends[e] - row0[e], 1) - 1, MOE_WIN.bit_length() - 1) + 1 for e in range(ne)]

    def more(k, c):
        floors = [row0[e] + k * MOE_WIN for e in range(ne)]
        rows = [jnp.minimum(f, last) for f in floors]
        for e in range(ne):
            @pl.when(k < nwin[e])
            def _():
                cp = window(rows[e], e, extra, xsem)
                cp.start()
                cp.wait()
        out_ref[...] += _dot(onehot(rows, floors), extra[...])
        return c

    lax.fori_loop(1, functools.reduce(jnp.maximum, nwin), more, 0)


def _combine(starts, y, posc, x, tt):
    n, ne = posc.shape
    nt = n // tt
    grid_spec = pltpu.PrefetchScalarGridSpec(
        num_scalar_prefetch=1,
        grid=(nt,),
        in_specs=[
            pl.BlockSpec(memory_space=pl.ANY),
            pl.BlockSpec((tt, ne), lambda t, st: (t, 0)),
            pl.BlockSpec((tt, D_MODEL), lambda t, st: (t, 0)),
        ],
        out_specs=pl.BlockSpec((tt, D_MODEL), lambda t, st: (t, 0)),
        scratch_shapes=[
            pltpu.VMEM((2, ne * MOE_WIN, D_MODEL), BF16),
            pltpu.VMEM((ne * MOE_WIN, D_MODEL), BF16),
            pltpu.SemaphoreType.DMA((2, ne)),
            pltpu.SemaphoreType.DMA(()),
        ],
    )
    return pl.pallas_call(
        functools.partial(_combine_kernel, nt=nt, cap=y.shape[1]),
        grid_spec=grid_spec,
        out_shape=jax.ShapeDtypeStruct((n, D_MODEL), F32),
        compiler_params=_params(1),
        name="ec_combine",
    )(starts, y, posc, x)


def _ffn_kernel(x_ref, wg_ref, wu_ref, wd_ref, y_ref, *, fc):
    x = x_ref[0, :, :D_MODEL]
    parts = x_ref[0, :, D_MODEL:].astype(F32)
    lane = lax.broadcasted_iota(jnp.int32, parts.shape, 1)
    gate = jnp.sum(jnp.where((lane % LANES) == pl.program_id(0), parts, 0.0), axis=1, keepdims=True)
    acc = jnp.zeros(y_ref.shape[1:], F32)
    for c in range(EXPERT_FF // fc):
        sl = slice(c * fc, (c + 1) * fc)
        a = _dot(x, wg_ref[0, :, sl])
        u = _dot(x, wu_ref[0, :, sl])
        hid = (a * (1.0 / (1.0 + jnp.exp(-a))) * u).astype(BF16)
        acc = acc + _dot(hid, wd_ref[0, sl, :])
    y_ref[0] = (acc * gate).astype(BF16)


def _ffn(xe, wg, wu, wd, cap, bm, fc):
    ne = wg.shape[0]
    return pl.pallas_call(
        functools.partial(_ffn_kernel, fc=fc),
        grid=(ne, cap // bm),
        in_specs=[
            pl.BlockSpec((1, bm, D_MODEL + GATE_LANES), lambda e, j: (e, j, 0)),
            pl.BlockSpec((1, D_MODEL, EXPERT_FF), lambda e, j: (e, 0, 0)),
            pl.BlockSpec((1, D_MODEL, EXPERT_FF), lambda e, j: (e, 0, 0)),
            pl.BlockSpec((1, EXPERT_FF, D_MODEL), lambda e, j: (e, 0, 0)),
        ],
        out_specs=pl.BlockSpec((1, bm, D_MODEL), lambda e, j: (e, j, 0)),
        out_shape=jax.ShapeDtypeStruct((ne, cap, D_MODEL), BF16),
        compiler_params=_params(2),
        name="ec_ffn",
    )(xe, wg, wu, wd)


def _tile(n, pref):
    return pref if n % pref == 0 else n


def _rope_tables(n):
    tok = jnp.arange(n, dtype=jnp.int32)
    pos = jnp.stack([tok // GRID_W, tok % GRID_W], axis=-1).astype(F32)
    inv = ROPE_THETA ** (-jnp.arange(0, ROPE_AXIS_DIM, 2, dtype=F32) / ROPE_AXIS_DIM)
    ang = pos[:, :, None] * inv
    cos, sin = jnp.cos(ang), jnp.sin(ang)
    half = ROPE_AXIS_DIM // 2
    cos_h = jnp.concatenate([cos, cos], axis=-1).reshape(n, HEAD_DIM)
    zeros = jnp.zeros_like(sin)
    s1_h = jnp.concatenate([zeros, sin], axis=-1).reshape(n, HEAD_DIM)
    s2_h = jnp.concatenate([-sin, zeros], axis=-1).reshape(n, HEAD_DIM)
    del half
    rep = lambda a: jnp.concatenate([a, a], axis=-1)
    return rep(cos_h), rep(s1_h), rep(s2_h)


def _block_diag_ones():
    idx = np.arange(MXU_DIM) // HEAD_DIM
    return jnp.asarray((idx[:, None] == idx[None, :]).astype(np.float32), BF16)


_A_GROUP = N_HEADS // A_KV_HEADS
_Q_PERM = np.array([kvp * 2 * _A_GROUP + half * _A_GROUP + g
                    for kvp in range(A_KV_HEADS // 2) for g in range(_A_GROUP) for half in range(2)])


def _perm_head_cols(w):
    d = w.shape[0]
    return w.reshape(d, N_HEADS, HEAD_DIM)[:, _Q_PERM].reshape(d, N_HEADS * HEAD_DIM)


def _moe(x1, hf, gates, afft, wg, wu, wd):
    n = x1.shape[0]
    cap = max(1, CAPACITY_FACTOR * n // N_EXPERTS)
    tt = _tile(n, MOE_TILE)
    bm = _tile(cap, 512)
    pos3, offs3 = _select(afft, cap)
    starts = jnp.concatenate([offs3[:, ::tt // LANES, 0], jnp.full((N_EXPERTS, 1), cap, jnp.int32)], axis=1)
    starts = starts.reshape(-1)
    posm = pos3.reshape(N_EXPERTS, n)
    xe = _dispatch(starts, hf, gates, posm, cap, tt)
    y = _ffn(xe, wg, wu, wd, cap, bm, 512)
    return _combine(starts, y, posm.T, x1, tt)


def _trunk(x, p):
    b, n, _ = x.shape
    t = b * n
    tm = _tile(n, 512)
    x2 = x.reshape(t, D_MODEL)
    for i in range(p["depth"]):
        j = i // 2
        g_mix = p["norm_mix"][i].reshape(1, D_MODEL)
        if i % 2 == 0:
            q, k, v = _gqa_qkv(x2, g_mix, p["a_w_qkv"][j], p["bd"], p["a_gains"][j], *p["rope"][n], n, tm)
            o = _gqa_attn(p["a_bound"][j], q, k, v, b, n, _tile(n, 256), _tile(n, 512))
            w_o = p["a_w_o"][j]
        else:
            q, k, v = _na_qkv(x2, g_mix, p["b_w_qkv"][j], p["bd"], p["b_gains"][j], tm)
            o = _na_attn(p["b_bound"][j], q, k, v, p["b_bias"][j], b, n, NA_SUB if (n // (NA_QB * GRID_W)) % NA_SUB == 0 else 4)
            w_o = p["b_w_o"][j]
        x1, hf, gates, afft = _proj_router(o, w_o, x2, p["norm_ffn"][i].reshape(1, D_MODEL),
                                          p["r_cat"][i], tm)
        x2 = _moe(x1, hf, gates, afft, p["wg"][i], p["wu"][i], p["wd"][i])
    return x2.reshape(b, n, D_MODEL)


def kernel(x_prompt, x_sample, norm_mix, norm_ffn, a_w_qkv, a_q_norm, a_k_norm, a_w_o, b_w_qkv, b_q_norm, b_k_norm,
           b_rpb, b_w_o, moe_router, moe_w_gate, moe_w_up, moe_w_down):
    depth = norm_mix.shape[0]
    dq = N_HEADS * HEAD_DIM
    n_a, n_b = a_w_qkv.shape[0], b_w_qkv.shape[0]
    a_w = jnp.concatenate([jnp.stack([_perm_head_cols(a_w_qkv[j, :, :dq]) for j in range(n_a)]),
                           a_w_qkv[:, :, dq:]], axis=-1).astype(BF16)
    a_wo = jnp.stack([a_w_o[j].reshape(N_HEADS, HEAD_DIM, D_MODEL)[_Q_PERM].reshape(dq, D_MODEL)
                      for j in range(n_a)]).astype(BF16)
    a_gains = jnp.concatenate([jnp.tile(a_q_norm, (1, N_HEADS)) * Q_GAIN_SCALE,
                               jnp.tile(a_k_norm, (1, A_KV_HEADS))], axis=-1).reshape(n_a, 1, -1)
    b_gains = jnp.concatenate([jnp.tile(b_q_norm, (1, N_HEADS)) * Q_GAIN_SCALE,
                               jnp.tile(b_k_norm, (1, N_HEADS))], axis=-1).reshape(n_b, 1, -1)
    r_pad = jnp.pad(moe_router, ((0, 0), (0, 0), (0, LANES - N_EXPERTS)))
    r_hi = r_pad.astype(BF16)
    r_lo = (r_pad - r_hi.astype(F32)).astype(BF16)
    a_bound = (HEAD_DIM * 1.01) * (jnp.max(jnp.abs(a_gains[:, 0, :dq]), axis=-1, keepdims=True)
                                   * jnp.max(jnp.abs(a_gains[:, 0, dq:]), axis=-1, keepdims=True))
    b_bound = (HEAD_DIM * 1.01) * (jnp.max(jnp.abs(b_gains[:, 0, :dq]), axis=-1, keepdims=True)
                                   * jnp.max(jnp.abs(b_gains[:, 0, dq:]), axis=-1, keepdims=True))
    b_bound = b_bound + LOG2_E * jnp.max(jnp.abs(b_rpb), axis=(1, 2, 3)).reshape(n_b, 1)
    p = {
        "depth": depth, "a_bound": a_bound, "b_bound": b_bound,
        "norm_mix": norm_mix, "norm_ffn": norm_ffn,
        "a_w_qkv": a_w, "a_w_o": a_wo, "a_gains": a_gains,
        "b_w_qkv": b_w_qkv.astype(BF16), "b_w_o": b_w_o.astype(BF16), "b_gains": b_gains,
        "b_bias": jnp.stack([_na_bias_table(b_rpb[j]) for j in range(n_b)]),
        "bd": _block_diag_ones(),
        "r_cat": jnp.concatenate([r_hi, r_lo], axis=-1),
        "wg": moe_w_gate.astype(BF16), "wu": moe_w_up.astype(BF16), "wd": moe_w_down.astype(BF16),
        "rope": {n: _rope_tables(n) for n in {x_prompt.shape[1], x_sample.shape[1]}},
    }
    return _trunk(x_prompt, p), _trunk(x_sample, p)
```

```python
import functools

import jax
import jax.numpy as jnp
import numpy as np
from jax import lax
from jax.experimental import pallas as pl
from jax.experimental.pallas import tpu as pltpu

D_MODEL = 1024
HEAD_DIM = 64
N_HEADS = 16
A_KV_HEADS = 4
GRID_W = 64
ROPE_THETA = 10000.0
ROPE_AXIS_DIM = HEAD_DIM // 2
NA_ROWS = 8
NA_KW = 16
NA_SUB = 8
NA_QB = 4
N_EXPERTS = 16
CAPACITY_FACTOR = 2
EXPERT_FF = 2048
RMS_EPS = 1e-6
NEG_INF = -1e30
ATTN_SCALE = HEAD_DIM ** -0.5
LOG2_E = float(np.log2(np.e))
Q_GAIN_SCALE = ATTN_SCALE * LOG2_E
SOFTMAX_SHIFT_LIMIT = 56.0
GQA_UNROLL = 8

LANES = 128
MXU_DIM = 256
VMEM_LIMIT = 48 * 1024 * 1024
ROW_ALIGN = 16
MOE_WIN = 128
MOE_TILE = 512
GATE_LANES = 2 * LANES
ROW_CHAIN = 256

BF16 = jnp.bfloat16
F32 = jnp.float32


def _params(n_axes, vmem=VMEM_LIMIT):
    return pltpu.CompilerParams(dimension_semantics=("arbitrary",) * n_axes, vmem_limit_bytes=vmem)


def _dot(a, b):
    return jnp.dot(a, b, preferred_element_type=F32)


def _dot_nt(a, b):
    return lax.dot_general(a, b, (((1,), (1,)), ((), ())), preferred_element_type=F32)


def _rms(x, g):
    ms = jnp.mean(x * x, axis=-1, keepdims=True)
    return x * lax.rsqrt(ms + RMS_EPS) * g


def _split_bf16(x):
    hi = x.astype(BF16)
    lo = (x - hi.astype(F32)).astype(BF16)
    return hi, lo


def _head_norm(y, bd, gains):
    sq = y * y
    hi, lo = _split_bf16(sq)
    w = y.shape[1]
    parts = []
    for c in range(w // MXU_DIM):
        sl = slice(c * MXU_DIM, (c + 1) * MXU_DIM)
        parts.append(_dot(hi[:, sl], bd) + _dot(lo[:, sl], bd))
    ss = jnp.concatenate(parts, axis=1)
    return y * lax.rsqrt(ss * (1.0 / HEAD_DIM) + RMS_EPS) * gains


def _gqa_qkv_kernel(x_ref, g_ref, w_ref, bd_ref, gains_ref, cos_ref, s1_ref, s2_ref, q_ref, k_ref, v_ref):
    dq = N_HEADS * HEAD_DIM
    dkv = A_KV_HEADS * HEAD_DIM
    half = ROPE_AXIS_DIM // 2
    for r in range(x_ref.shape[0] // ROW_CHAIN):
        rows = slice(r * ROW_CHAIN, (r + 1) * ROW_CHAIN)
        h = _rms(x_ref[rows, :], g_ref[...]).astype(BF16)
        qkv = _dot(h, w_ref[...])
        qk = _head_norm(qkv[:, :dq + dkv], bd_ref[...], gains_ref[...])
        cos, s1, s2 = cos_ref[rows, :], s1_ref[rows, :], s2_ref[rows, :]
        outs = []
        for c in range((dq + dkv) // LANES):
            xc = qk[:, c * LANES:(c + 1) * LANES]
            outs.append(xc * cos + pltpu.roll(xc, half, 1) * s1 + pltpu.roll(xc, LANES - half, 1) * s2)
        roped = jnp.concatenate(outs, axis=1)
        q_ref[rows, :] = roped[:, :dq].astype(BF16)
        k_ref[0, :, rows] = roped[:, dq:].T.astype(BF16)
        v_ref[rows, :] = qkv[:, dq + dkv:].astype(BF16)


def _gqa_qkv(x, g, w, bd, gains, cos, s1, s2, n, tm):
    t = x.shape[0]
    dq = N_HEADS * HEAD_DIM
    dkv = A_KV_HEADS * HEAD_DIM
    npos = n // tm
    full = lambda shape: pl.BlockSpec(shape, lambda i: (0, 0))
    return pl.pallas_call(
        _gqa_qkv_kernel,
        grid=(t // tm,),
        in_specs=[
            pl.BlockSpec((tm, D_MODEL), lambda i: (i, 0)),
            full((1, D_MODEL)),
            full((D_MODEL, dq + 2 * dkv)),
            full((MXU_DIM, MXU_DIM)),
            full((1, dq + dkv)),
            pl.BlockSpec((tm, LANES), lambda i: (i % npos, 0)),
            pl.BlockSpec((tm, LANES), lambda i: (i % npos, 0)),
            pl.BlockSpec((tm, LANES), lambda i: (i % npos, 0)),
        ],
        out_specs=[
            pl.BlockSpec((tm, dq), lambda i: (i, 0)),
            pl.BlockSpec((1, dkv, tm), lambda i: (i // npos, 0, i % npos)),
            pl.BlockSpec((tm, dkv), lambda i: (i, 0)),
        ],
        out_shape=[
            jax.ShapeDtypeStruct((t, dq), BF16),
            jax.ShapeDtypeStruct((t // n, dkv, n), BF16),
            jax.ShapeDtypeStruct((t, dkv), BF16),
        ],
        compiler_params=_params(1),
        name="gqa_qkv",
    )(x, g, w, bd, gains, cos, s1, s2)


def _gqa_attn_kernel(bound_ref, q_ref, k_ref, v_ref, o_ref, *, n, tk, unroll):
    tq = q_ref.shape[0]
    lane = lax.broadcasted_iota(jnp.int32, (tq, LANES), 1)
    low = lane < HEAD_DIM
    bound = bound_ref[0]

    def key_tile(j):
        start = pl.multiple_of(j * tk, tk)
        return k_ref[0, :, pl.ds(start, tk)], v_ref[pl.ds(start, tk), :]

    for c in range(q_ref.shape[1] // LANES):
        cols = slice(c * LANES, (c + 1) * LANES)
        qc = q_ref[:, cols]
        zero = jnp.zeros_like(qc)
        qa = jnp.where(low, qc, zero)
        qb = jnp.where(low, zero, qc)

        @pl.when(bound <= SOFTMAX_SHIFT_LIMIT)
        def _():
            def body(j, carry):
                ls_a, ls_b, acc = carry
                kt, vt = key_tile(j)
                p_a = jnp.exp2(_dot(qa, kt) - bound)
                p_b = jnp.exp2(_dot(qb, kt) - bound)
                for i in range(tk // LANES):
                    ls_a = ls_a + p_a[:, i * LANES:(i + 1) * LANES]
                    ls_b = ls_b + p_b[:, i * LANES:(i + 1) * LANES]
                pv_a = _dot(p_a.astype(BF16), vt)
                pv_b = _dot(p_b.astype(BF16), vt)
                return ls_a, ls_b, acc + jnp.where(low, pv_a, pv_b)

            z = jnp.zeros((tq, LANES), F32)
            ls_a, ls_b, acc = lax.fori_loop(0, n // tk, body, (z, z, z), unroll=unroll)
            l_a = jnp.sum(ls_a, axis=1, keepdims=True)
            l_b = jnp.sum(ls_b, axis=1, keepdims=True)
            o_ref[:, cols] = (acc / jnp.where(low, l_a, l_b)).astype(BF16)

        @pl.when(bound > SOFTMAX_SHIFT_LIMIT)
        def _():
            def body(j, carry):
                m_a, l_a, m_b, l_b, acc = carry
                kt, vt = key_tile(j)
                s_a = _dot(qa, kt)
                s_b = _dot(qb, kt)
                mn_a = jnp.maximum(m_a, jnp.max(s_a, axis=-1, keepdims=True))
                mn_b = jnp.maximum(m_b, jnp.max(s_b, axis=-1, keepdims=True))
                p_a = jnp.exp2(s_a - mn_a)
                p_b = jnp.exp2(s_b - mn_b)
                al_a = jnp.exp2(m_a - mn_a)
                al_b = jnp.exp2(m_b - mn_b)
                l_a = al_a * l_a + jnp.sum(p_a, axis=-1, keepdims=True)
                l_b = al_b * l_b + jnp.sum(p_b, axis=-1, keepdims=True)
                pv_a = _dot(p_a.astype(BF16), vt)
                pv_b = _dot(p_b.astype(BF16), vt)
                acc = acc * jnp.where(low, al_a, al_b) + jnp.where(low, pv_a, pv_b)
                return mn_a, l_a, mn_b, l_b, acc

            neg = jnp.full((tq, 1), NEG_INF, F32)
            zer = jnp.zeros((tq, 1), F32)
            _, l_a, _, l_b, acc = lax.fori_loop(0, n // tk, body, (neg, zer, neg, zer, jnp.zeros((tq, LANES), F32)))
            o_ref[:, cols] = (acc / jnp.where(low, l_a, l_b)).astype(BF16)


def _gqa_attn(bound, q, k, v, b, n, tq, tk):
    t = q.shape[0]
    dq = N_HEADS * HEAD_DIM
    nq = n // tq
    pairs = A_KV_HEADS // 2
    qw = dq // pairs
    return pl.pallas_call(
        functools.partial(_gqa_attn_kernel, n=n, tk=tk, unroll=min(n // tk, GQA_UNROLL)),
        grid=(b, pairs, nq),
        in_specs=[
            pl.BlockSpec(memory_space=pltpu.SMEM),
            pl.BlockSpec((tq, qw), lambda bi, p, i: (bi * nq + i, p)),
            pl.BlockSpec((1, LANES, n), lambda bi, p, i: (bi, p, 0)),
            pl.BlockSpec((n, LANES), lambda bi, p, i: (bi, p)),
        ],
        out_specs=pl.BlockSpec((tq, qw), lambda bi, p, i: (bi * nq + i, p)),
        out_shape=jax.ShapeDtypeStruct((t, dq), BF16),
        compiler_params=_params(3),
        name="gqa_attn",
    )(bound, q, k, v)


def _na_qkv_kernel(x_ref, g_ref, w_ref, bd_ref, gains_ref, q_ref, k_ref, v_ref):
    dh = N_HEADS * HEAD_DIM
    h = _rms(x_ref[...], g_ref[...]).astype(BF16)
    qkv = _dot(h, w_ref[...])
    qk = _head_norm(qkv[:, :2 * dh], bd_ref[...], gains_ref[...])
    q_ref[...] = qk[:, :dh].astype(BF16)
    k_ref[...] = qk[:, dh:].astype(BF16)
    v_ref[...] = qkv[:, 2 * dh:].astype(BF16)


def _na_qkv(x, g, w, bd, gains, tm):
    t = x.shape[0]
    dh = N_HEADS * HEAD_DIM
    full = lambda shape: pl.BlockSpec(shape, lambda i: (0, 0))
    return pl.pallas_call(
        _na_qkv_kernel,
        grid=(t // tm,),
        in_specs=[
            pl.BlockSpec((tm, D_MODEL), lambda i: (i, 0)),
            full((1, D_MODEL)),
            full((D_MODEL, 3 * dh)),
            full((MXU_DIM, MXU_DIM)),
            full((1, 2 * dh)),
        ],
        out_specs=[pl.BlockSpec((tm, dh), lambda i: (i, 0))] * 3,
        out_shape=[jax.ShapeDtypeStruct((t, dh), BF16)] * 3,
        compiler_params=_params(1),
        name="na_qkv",
    )(x, g, w, bd, gains)


def _na_attn_kernel(bound_ref, q_ref, k_ref, v_ref, bias_ref, o_ref, *, nblk, sub):
    ub = pl.program_id(2)
    blk = NA_QB * GRID_W
    win = 3 * blk
    lane = lax.broadcasted_iota(jnp.int32, (blk, LANES), 1)
    low = lane < HEAD_DIM
    bound = bound_ref[0]

    def block(s, softmax):
        u = ub * sub + s
        var = jnp.where(u == 0, 0, jnp.where(u == nblk - 1, 2, 1))
        start = pl.multiple_of(jnp.clip(u - 1, 0, nblk - 3) * blk, blk)
        kw = k_ref[pl.ds(start, win), :]
        vw = v_ref[pl.ds(start, win), :]
        qs = q_ref[s * blk:(s + 1) * blk, :]
        zero = jnp.zeros_like(qs)
        halves = []
        for hh, qm in enumerate((jnp.where(low, qs, zero), jnp.where(low, zero, qs))):
            p, l = softmax(_dot_nt(qm, kw) + bias_ref[var, hh])
            halves.append(_dot(p.astype(BF16), vw) / l)
        o_ref[s * blk:(s + 1) * blk, :] = jnp.where(low, halves[0], halves[1]).astype(BF16)

    def shifted(sc):
        p = jnp.exp2(sc - bound)
        ls = p[:, :LANES]
        for i in range(1, win // LANES):
            ls = ls + p[:, i * LANES:(i + 1) * LANES]
        return p, jnp.sum(ls, axis=1, keepdims=True)

    def running_max(sc):
        p = jnp.exp2(sc - jnp.max(sc, axis=-1, keepdims=True))
        return p, jnp.sum(p, axis=-1, keepdims=True)

    @pl.when(bound <= SOFTMAX_SHIFT_LIMIT)
    def _():
        for s in range(sub):
            block(s, shifted)

    @pl.when(bound > SOFTMAX_SHIFT_LIMIT)
    def _():
        for s in range(sub):
            block(s, running_max)


def _na_attn(bound, q, k, v, bias, b, n, sub):
    t = q.shape[0]
    dh = N_HEADS * HEAD_DIM
    blk = NA_QB * GRID_W
    nblk = n // blk
    assert nblk >= 3 and nblk % sub == 0
    nstep = nblk // sub
    tq = sub * blk
    return pl.pallas_call(
        functools.partial(_na_attn_kernel, nblk=nblk, sub=sub),
        grid=(N_HEADS // 2, b, nstep),
        in_specs=[
            pl.BlockSpec(memory_space=pltpu.SMEM),
            pl.BlockSpec((tq, LANES), lambda p, bi, i: (bi * nstep + i, p)),
            pl.BlockSpec((n, LANES), lambda p, bi, i: (bi, p)),
            pl.BlockSpec((n, LANES), lambda p, bi, i: (bi, p)),
            pl.BlockSpec((3, 2, blk, 3 * blk), lambda p, bi, i: (0, p, 0, 0)),
        ],
        out_specs=pl.BlockSpec((tq, LANES), lambda p, bi, i: (bi * nstep + i, p)),
        out_shape=jax.ShapeDtypeStruct((t, dh), BF16),
        compiler_params=_params(3),
        name="na_attn",
    )(bound, q, k, v, bias)


def _na_bias_table(rpb):
    nkr = 3 * NA_QB
    i = np.arange(NA_QB)[:, None]
    kr = np.arange(nkr)[None, :]
    variants = [(0 * i, NA_ROWS - 1), (i, NA_ROWS - 1 - NA_QB), (NA_QB + 0 * i, NA_ROWS - 1 - 2 * NA_QB)]
    row_valid = np.stack([(kr >= r0) & (kr < r0 + NA_ROWS) for r0, _ in variants])
    dr = np.stack([np.clip(kr - i + off, 0, 2 * NA_ROWS - 2) for _, off in variants])
    row_sel = (dr[..., None] == np.arange(2 * NA_ROWS - 1)).astype(np.float32)
    qc = np.arange(GRID_W)[:, None]
    kc = np.arange(GRID_W)[None, :]
    win0 = np.clip(qc - NA_KW // 2, 0, GRID_W - NA_KW)
    col_valid = (kc >= win0) & (kc < win0 + NA_KW)
    dc = np.clip(kc - qc + NA_KW - 1, 0, 2 * NA_KW - 2)
    col_sel = (dc[..., None] == np.arange(2 * NA_KW - 1)).astype(np.float32)
    hp = lax.Precision.HIGHEST
    tmp = jnp.einsum("vikd,hde->vhike", row_sel, rpb.astype(F32), precision=hp)
    tbl = jnp.einsum("vhike,qce->vhiqkc", tmp, col_sel, precision=hp)
    valid = row_valid[:, :, None, :, None] & col_valid[None, None, :, None, :]
    tbl = jnp.where(valid[:, None], tbl * LOG2_E, NEG_INF)
    return tbl.reshape(3, N_HEADS, NA_QB * GRID_W, nkr * GRID_W)


def _proj_router_kernel(o_ref, w_ref, x_ref, g_ref, rcat_ref, x1_ref, hf_ref, gate_ref, afft_ref):
    tm = o_ref.shape[0]
    for r in range(tm // ROW_CHAIN):
        rows = slice(r * ROW_CHAIN, (r + 1) * ROW_CHAIN)
        x1 = x_ref[rows, :] + _dot(o_ref[rows, :], w_ref[...])
        x1_ref[rows, :] = x1
        hf = _rms(x1, g_ref[...])
        hi, lo = _split_bf16(hf)
        hf_ref[rows, :] = hi
        both = _dot(hi, rcat_ref[...])
        logits = both[:, :LANES] + (both[:, LANES:] + _dot(lo, rcat_ref[:, :LANES]))
        lane = lax.broadcasted_iota(jnp.int32, logits.shape, 1)
        logits = jnp.where(lane < N_EXPERTS, logits, NEG_INF)
        m = jnp.max(logits, axis=-1, keepdims=True)
        e = jnp.exp(logits - m)
        aff = e / jnp.sum(e, axis=-1, keepdims=True)
        g_hi, g_lo = _split_bf16(aff)
        gate_ref[rows, :LANES] = g_hi
        gate_ref[rows, LANES:] = g_lo
        afft_ref[:, rows] = aff.T[:N_EXPERTS, :]


def _proj_router(o, w_o, x, g, r_cat, tm):
    t = x.shape[0]
    full = lambda shape: pl.BlockSpec(shape, lambda i: (0, 0))
    return pl.pallas_call(
        _proj_router_kernel,
        grid=(t // tm,),
        in_specs=[
            pl.BlockSpec((tm, D_MODEL), lambda i: (i, 0)),
            full((D_MODEL, D_MODEL)),
            pl.BlockSpec((tm, D_MODEL), lambda i: (i, 0)),
            full((1, D_MODEL)),
            full((D_MODEL, 2 * LANES)),
        ],
        out_specs=[
            pl.BlockSpec((tm, D_MODEL), lambda i: (i, 0)),
            pl.BlockSpec((tm, D_MODEL), lambda i: (i, 0)),
            pl.BlockSpec((tm, GATE_LANES), lambda i: (i, 0)),
            pl.BlockSpec((N_EXPERTS, tm), lambda i: (0, i)),
        ],
        out_shape=[
            jax.ShapeDtypeStruct((t, D_MODEL), F32),
            jax.ShapeDtypeStruct((t, D_MODEL), BF16),
            jax.ShapeDtypeStruct((t, GATE_LANES), BF16),
            jax.ShapeDtypeStruct((N_EXPERTS, t), F32),
        ],
        compiler_params=_params(1),
        name="proj_router",
    )(o, w_o, x, g, r_cat)


def _select_kernel(aff_ref, upper_ref, lower_ref, pos_ref, offs_ref, *, cap):
    aff = aff_ref[...]
    ne, nr, _ = aff.shape

    def count(mask):
        return jnp.sum(jnp.sum(jnp.where(mask, 1.0, 0.0), axis=2, keepdims=True), axis=1, keepdims=True)

    def bisect(i, bits):
        cand = bits | jnp.left_shift(jnp.int32(1), 30 - i)
        return jnp.where(count(aff >= pltpu.bitcast(cand, F32)) >= cap, cand, bits)

    thr = pltpu.bitcast(lax.fori_loop(0, 31, bisect, jnp.zeros((ne, 1, 1), jnp.int32)), F32)
    above = aff > thr
    tied = aff == thr
    need = cap - count(above)

    ones = jnp.ones((LANES, LANES), BF16)

    def prefix(mask):
        mb = jnp.where(mask, 1.0, 0.0).astype(BF16).reshape(ne * nr, LANES)
        incl = _dot(mb, upper_ref[...]).reshape(ne, nr, LANES)
        tot = _dot(mb, ones).astype(BF16).reshape(ne, nr, LANES)
        offs = jnp.stack([_dot(lower_ref[...], tot[e]) for e in range(ne)], axis=0)
        return offs + incl - mb.astype(F32).reshape(ne, nr, LANES), offs

    tie_rank, _ = prefix(tied)
    sel = above | (tied & (tie_rank < need))
    pos, offs = prefix(sel)
    pos_ref[...] = jnp.where(sel, pos, -1.0).astype(jnp.int32)
    offs_ref[...] = offs.astype(jnp.int32)


def _select(afft, cap):
    ne, n = afft.shape
    nr = n // LANES
    aff3 = afft.reshape(ne, nr, LANES)
    upper = jnp.asarray(np.triu(np.ones((LANES, LANES), np.float32)), BF16)
    lower = jnp.asarray(np.tril(np.ones((nr, nr), np.float32), -1), BF16)
    blk3 = pl.BlockSpec((ne, nr, LANES), lambda i: (0, 0, 0))
    return pl.pallas_call(
        functools.partial(_select_kernel, cap=cap),
        grid=(1,),
        in_specs=[blk3, pl.BlockSpec((LANES, LANES), lambda i: (0, 0)), pl.BlockSpec((nr, nr), lambda i: (0, 0))],
        out_specs=[blk3, blk3],
        out_shape=[jax.ShapeDtypeStruct((ne, nr, LANES), jnp.int32)] * 2,
        compiler_params=_params(1),
        name="ec_select",
    )(aff3, upper, lower)


def _align_down(x):
    return lax.shift_left(lax.shift_right_logical(x, ROW_ALIGN.bit_length() - 1), ROW_ALIGN.bit_length() - 1)


def _window_info(st_ref, e, t, nt):
    start = st_ref[e * (nt + 1) + t]
    end = st_ref[e * (nt + 1) + t + 1]
    base = _align_down(start)
    return base, end, lax.shift_right_logical(end - base, MOE_WIN.bit_length() - 1) + 1


def _dispatch_kernel(st_ref, hf_ref, gate_ref, pos_ref, xe_ref, stage, extra, carry, sems, xsem, *, nt, cap):
    t = pl.program_id(0)
    ne, tt = pos_ref.shape
    slot = t % 2
    hf = jnp.concatenate([hf_ref[...], gate_ref[...]], axis=1)
    rows = lax.broadcasted_iota(jnp.int32, (MOE_WIN, tt), 0)
    info = [_window_info(st_ref, e, t, nt) for e in range(ne)]

    @pl.when(t == 0)
    def _():
        carry[...] = jnp.zeros_like(carry)
        extra[...] = jnp.zeros_like(extra)
        pads = [pltpu.make_async_copy(extra.at[pl.ds(e * MOE_WIN, MOE_WIN), :], xe_ref.at[e, pl.ds(cap, MOE_WIN), :], xsem)
                for e in range(ne)]
        for cp in pads:
            cp.start()
        for cp in pads:
            cp.wait()

    def gathered(k):
        parts = []
        for e in range(ne):
            rel = pos_ref[e:e + 1, :] - (info[e][0] + k * MOE_WIN)
            parts.append(jnp.where(rel == rows, 1.0, 0.0).astype(BF16))
        return _dot(jnp.concatenate(parts, axis=0), hf).astype(BF16)

    def window(buf, e, k, sem):
        row = pl.multiple_of(info[e][0] + k * MOE_WIN, ROW_ALIGN)
        return pltpu.make_async_copy(buf.at[pl.ds(e * MOE_WIN, MOE_WIN), :], xe_ref.at[e, pl.ds(row, MOE_WIN), :], sem)

    stage[slot] = gathered(0)
    for e in range(ne):
        base, end, nwin = info[e]
        head = pl.ds(e * MOE_WIN, ROW_ALIGN)
        stage[slot, head, :] = stage[slot, head, :] + carry[e]

        @pl.when(nwin == 1)
        def _():
            tail = pl.multiple_of(e * MOE_WIN + _align_down(end) - base, ROW_ALIGN)
            carry[e] = stage[slot, pl.ds(tail, ROW_ALIGN), :]

        @pl.when(t > 0)
        def _():
            window(stage.at[1 - slot], e, 0, sems.at[e]).wait()

        window(stage.at[slot], e, 0, sems.at[e]).start()

    def more(k, c):
        extra[...] = gathered(k)
        for e in range(ne):
            base, end, nwin = info[e]

            @pl.when(k < nwin)
            def _():
                cp = window(extra, e, k, xsem)
                cp.start()
                cp.wait()

            @pl.when(k == nwin - 1)
            def _():
                tail = pl.multiple_of(e * MOE_WIN + _align_down(end) - base - k * MOE_WIN, ROW_ALIGN)
                carry[e] = extra[pl.ds(tail, ROW_ALIGN), :]
        return c

    lax.fori_loop(1, functools.reduce(jnp.maximum, [i[2] for i in info]), more, 0)

    @pl.when(t == nt - 1)
    def _():
        for e in range(ne):
            window(stage.at[slot], e, 0, sems.at[e]).wait()


def _dispatch(starts, hf, gates, posm, cap, tt):
    ne, n = posm.shape
    nt = n // tt
    width = D_MODEL + GATE_LANES
    grid_spec = pltpu.PrefetchScalarGridSpec(
        num_scalar_prefetch=1,
        grid=(nt,),
        in_specs=[
            pl.BlockSpec((tt, D_MODEL), lambda t, st: (t, 0)),
            pl.BlockSpec((tt, GATE_LANES), lambda t, st: (t, 0)),
            pl.BlockSpec((ne, tt), lambda t, st: (0, t)),
        ],
        out_specs=pl.BlockSpec(memory_space=pl.ANY),
        scratch_shapes=[
            pltpu.VMEM((2, ne * MOE_WIN, width), BF16),
            pltpu.VMEM((ne * MOE_WIN, width), BF16),
            pltpu.VMEM((ne, ROW_ALIGN, width), BF16),
            pltpu.SemaphoreType.DMA((ne,)),
            pltpu.SemaphoreType.DMA(()),
        ],
    )
    return pl.pallas_call(
        functools.partial(_dispatch_kernel, nt=nt, cap=cap),
        grid_spec=grid_spec,
        out_shape=jax.ShapeDtypeStruct((ne, cap + MOE_WIN, width), BF16),
        compiler_params=_params(1),
        name="ec_dispatch",
    )(starts, hf, gates, posm)


def _combine_kernel(st_ref, y_ref, posc_ref, x_ref, out_ref, ybuf, extra, sems, xsem, *, nt, cap):
    t = pl.program_id(0)
    tt, ne = posc_ref.shape
    slot = t % 2
    last = cap - MOE_WIN

    def main_row(tile, e):
        return jnp.minimum(_window_info(st_ref, e, tile, nt)[0], last)

    def window(row, e, buf, sem):
        src = y_ref.at[e, pl.ds(pl.multiple_of(row, ROW_ALIGN), MOE_WIN), :]
        return pltpu.make_async_copy(src, buf.at[pl.ds(e * MOE_WIN, MOE_WIN), :], sem)

    @pl.when(t == 0)
    def _():
        extra[...] = jnp.zeros_like(extra)
        for e in range(ne):
            window(main_row(0, e), e, ybuf.at[0], sems.at[0, e]).start()

    @pl.when(t + 1 < nt)
    def _():
        for e in range(ne):
            window(main_row(t + 1, e), e, ybuf.at[1 - slot], sems.at[1 - slot, e]).start()

    row0 = [main_row(t, e) for e in range(ne)]
    for e in range(ne):
        window(row0[e], e, ybuf.at[slot], sems.at[slot, e]).wait()

    def onehot(rows, floors, toks=slice(None)):
        parts = []
        for e in range(ne):
            pos = posc_ref[toks, e:e + 1]
            hit = (pos - rows[e]) == lax.broadcasted_iota(jnp.int32, (pos.shape[0], MOE_WIN), 1)
            if floors is not None:
                hit = jnp.logical_and(hit, pos >= floors[e])
            parts.append(jnp.where(hit, 1.0, 0.0).astype(BF16))
        return jnp.concatenate(parts, axis=1)

    y = ybuf[slot]
    for r in range(tt // ROW_CHAIN):
        rows = slice(r * ROW_CHAIN, (r + 1) * ROW_CHAIN)
        out_ref[rows, :] = x_ref[rows, :] + _dot(onehot(row0, None, rows), y)

    ends = [_window_info(st_ref, e, t, nt)[1] for e in range(ne)]
    nwin = [lax.shift_right_logical(jnp.maximum(ends[e] - row0[e], 1) - 1, MOE_WIN.bit_length() - 1) + 1 for e in range(ne)]

    def more(k, c):
        floors = [row0[e] + k * MOE_WIN for e in range(ne)]
        rows = [jnp.minimum(f, last) for f in floors]
        for e in range(ne):
            @pl.when(k < nwin[e])
            def _():
                cp = window(rows[e], e, extra, xsem)
                cp.start()
                cp.wait()
        out_ref[...] += _dot(onehot(rows, floors), extra[...])
        return c

    lax.fori_loop(1, functools.reduce(jnp.maximum, nwin), more, 0)


def _combine(starts, y, posc, x, tt):
    n, ne = posc.shape
    nt = n // tt
    grid_spec = pltpu.PrefetchScalarGridSpec(
        num_scalar_prefetch=1,
        grid=(nt,),
        in_specs=[
            pl.BlockSpec(memory_space=pl.ANY),
            pl.BlockSpec((tt, ne), lambda t, st: (t, 0)),
            pl.BlockSpec((tt, D_MODEL), lambda t, st: (t, 0)),
        ],
        out_specs=pl.BlockSpec((tt, D_MODEL), lambda t, st: (t, 0)),
        scratch_shapes=[
            pltpu.VMEM((2, ne * MOE_WIN, D_MODEL), BF16),
            pltpu.VMEM((ne * MOE_WIN, D_MODEL), BF16),
            pltpu.SemaphoreType.DMA((2, ne)),
            pltpu.SemaphoreType.DMA(()),
        ],
    )
    return pl.pallas_call(
        functools.partial(_combine_kernel, nt=nt, cap=y.shape[1]),
        grid_spec=grid_spec,
        out_shape=jax.ShapeDtypeStruct((n, D_MODEL), F32),
        compiler_params=_params(1),
        name="ec_combine",
    )(starts, y, posc, x)


def _ffn_kernel(x_ref, wg_ref, wu_ref, wd_ref, y_ref, *, fc):
    x = x_ref[0, :, :D_MODEL]
    parts = x_ref[0, :, D_MODEL:].astype(F32)
    lane = lax.broadcasted_iota(jnp.int32, parts.shape, 1)
    gate = jnp.sum(jnp.where((lane % LANES) == pl.program_id(0), parts, 0.0), axis=1, keepdims=True)
    acc = jnp.zeros(y_ref.shape[1:], F32)
    for c in range(EXPERT_FF // fc):
        sl = slice(c * fc, (c + 1) * fc)
        a = _dot(x, wg_ref[0, :, sl])
        u = _dot(x, wu_ref[0, :, sl])
        hid = (a * (1.0 / (1.0 + jnp.exp(-a))) * u).astype(BF16)
        acc = acc + _dot(hid, wd_ref[0, sl, :])
    y_ref[0] = (acc * gate).astype(BF16)


def _ffn(xe, wg, wu, wd, cap, bm, fc):
    ne = wg.shape[0]
    return pl.pallas_call(
        functools.partial(_ffn_kernel, fc=fc),
        grid=(ne, cap // bm),
        in_specs=[
            pl.BlockSpec((1, bm, D_MODEL + GATE_LANES), lambda e, j: (e, j, 0)),
            pl.BlockSpec((1, D_MODEL, EXPERT_FF), lambda e, j: (e, 0, 0)),
            pl.BlockSpec((1, D_MODEL, EXPERT_FF), lambda e, j: (e, 0, 0)),
            pl.BlockSpec((1, EXPERT_FF, D_MODEL), lambda e, j: (e, 0, 0)),
        ],
        out_specs=pl.BlockSpec((1, bm, D_MODEL), lambda e, j: (e, j, 0)),
        out_shape=jax.ShapeDtypeStruct((ne, cap, D_MODEL), BF16),
        compiler_params=_params(2),
        name="ec_ffn",
    )(xe, wg, wu, wd)


def _tile(n, pref):
    return pref if n % pref == 0 else n


def _rope_tables(n):
    tok = jnp.arange(n, dtype=jnp.int32)
    pos = jnp.stack([tok // GRID_W, tok % GRID_W], axis=-1).astype(F32)
    inv = ROPE_THETA ** (-jnp.arange(0, ROPE_AXIS_DIM, 2, dtype=F32) / ROPE_AXIS_DIM)
    ang = pos[:, :, None] * inv
    cos, sin = jnp.cos(ang), jnp.sin(ang)
    half = ROPE_AXIS_DIM // 2
    cos_h = jnp.concatenate([cos, cos], axis=-1).reshape(n, HEAD_DIM)
    zeros = jnp.zeros_like(sin)
    s1_h = jnp.concatenate([zeros, sin], axis=-1).reshape(n, HEAD_DIM)
    s2_h = jnp.concatenate([-sin, zeros], axis=-1).reshape(n, HEAD_DIM)
    del half
    rep = lambda a: jnp.concatenate([a, a], axis=-1)
    return rep(cos_h), rep(s1_h), rep(s2_h)


def _block_diag_ones():
    idx = np.arange(MXU_DIM) // HEAD_DIM
    return jnp.asarray((idx[:, None] == idx[None, :]).astype(np.float32), BF16)


_A_GROUP = N_HEADS // A_KV_HEADS
_Q_PERM = np.array([kvp * 2 * _A_GROUP + half * _A_GROUP + g
                    for kvp in range(A_KV_HEADS // 2) for g in range(_A_GROUP) for half in range(2)])


def _perm_head_cols(w):
    d = w.shape[0]
    return w.reshape(d, N_HEADS, HEAD_DIM)[:, _Q_PERM].reshape(d, N_HEADS * HEAD_DIM)


def _moe(x1, hf, gates, afft, wg, wu, wd):
    n = x1.shape[0]
    cap = max(1, CAPACITY_FACTOR * n // N_EXPERTS)
    tt = _tile(n, MOE_TILE)
    bm = _tile(cap, 512)
    pos3, offs3 = _select(afft, cap)
    starts = jnp.concatenate([offs3[:, ::tt // LANES, 0], jnp.full((N_EXPERTS, 1), cap, jnp.int32)], axis=1)
    starts = starts.reshape(-1)
    posm = pos3.reshape(N_EXPERTS, n)
    xe = _dispatch(starts, hf, gates, posm, cap, tt)
    y = _ffn(xe, wg, wu, wd, cap, bm, 512)
    return _combine(starts, y, posm.T, x1, tt)


def _trunk(x, p):
    b, n, _ = x.shape
    t = b * n
    tm = _tile(n, 512)
    x2 = x.reshape(t, D_MODEL)
    for i in range(p["depth"]):
        j = i // 2
        g_mix = p["norm_mix"][i].reshape(1, D_MODEL)
        if i % 2 == 0:
            q, k, v = _gqa_qkv(x2, g_mix, p["a_w_qkv"][j], p["bd"], p["a_gains"][j], *p["rope"][n], n, tm)
            o = _gqa_attn(p["a_bound"][j], q, k, v, b, n, _tile(n, 256), _tile(n, 512))
            w_o = p["a_w_o"][j]
        else:
            q, k, v = _na_qkv(x2, g_mix, p["b_w_qkv"][j], p["bd"], p["b_gains"][j], tm)
            o = _na_attn(p["b_bound"][j], q, k, v, p["b_bias"][j], b, n, NA_SUB if (n // (NA_QB * GRID_W)) % NA_SUB == 0 else 4)
            w_o = p["b_w_o"][j]
        x1, hf, gates, afft = _proj_router(o, w_o, x2, p["norm_ffn"][i].reshape(1, D_MODEL),
                                          p["r_cat"][i], tm)
        x2 = _moe(x1, hf, gates, afft, p["wg"][i], p["wu"][i], p["wd"][i])
    return x2.reshape(b, n, D_MODEL)


def kernel(x_prompt, x_sample, norm_mix, norm_ffn, a_w_qkv, a_q_norm, a_k_norm, a_w_o, b_w_qkv, b_q_norm, b_k_norm,
           b_rpb, b_w_o, moe_router, moe_w_gate, moe_w_up, moe_w_down):
    depth = norm_mix.shape[0]
    dq = N_HEADS * HEAD_DIM
    n_a, n_b = a_w_qkv.shape[0], b_w_qkv.shape[0]
    a_w = jnp.concatenate([jnp.stack([_perm_head_cols(a_w_qkv[j, :, :dq]) for j in range(n_a)]),
                           a_w_qkv[:, :, dq:]], axis=-1).astype(BF16)
    a_wo = jnp.stack([a_w_o[j].reshape(N_HEADS, HEAD_DIM, D_MODEL)[_Q_PERM].reshape(dq, D_MODEL)
                      for j in range(n_a)]).astype(BF16)
    a_gains = jnp.concatenate([jnp.tile(a_q_norm, (1, N_HEADS)) * Q_GAIN_SCALE,
                               jnp.tile(a_k_norm, (1, A_KV_HEADS))], axis=-1).reshape(n_a, 1, -1)
    b_gains = jnp.concatenate([jnp.tile(b_q_norm, (1, N_HEADS)) * Q_GAIN_SCALE,
                               jnp.tile(b_k_norm, (1, N_HEADS))], axis=-1).reshape(n_b, 1, -1)
    r_pad = jnp.pad(moe_router, ((0, 0), (0, 0), (0, LANES - N_EXPERTS)))
    r_hi = r_pad.astype(BF16)
    r_lo = (r_pad - r_hi.astype(F32)).astype(BF16)
    a_bound = (HEAD_DIM * 1.01) * (jnp.max(jnp.abs(a_gains[:, 0, :dq]), axis=-1, keepdims=True)
                                   * jnp.max(jnp.abs(a_gains[:, 0, dq:]), axis=-1, keepdims=True))
    b_bound = (HEAD_DIM * 1.01) * (jnp.max(jnp.abs(b_gains[:, 0, :dq]), axis=-1, keepdims=True)
                                   * jnp.max(jnp.abs(b_gains[:, 0, dq:]), axis=-1, keepdims=True))
    b_bound = b_bound + LOG2_E * jnp.max(jnp.abs(b_rpb), axis=(1, 2, 3)).reshape(n_b, 1)
    p = {
        "depth": depth, "a_bound": a_bound, "b_bound": b_bound,
        "norm_mix": norm_mix, "norm_ffn": norm_ffn,
        "a_w_qkv": a_w, "a_w_o": a_wo, "a_gains": a_gains,
        "b_w_qkv": b_w_qkv.astype(BF16), "b_w_o": b_w_o.astype(BF16), "b_gains": b_gains,
        "b_bias": jnp.stack([_na_bias_table(b_rpb[j]) for j in range(n_b)]),
        "bd": _block_diag_ones(),
        "r_cat": jnp.concatenate([r_hi, r_lo], axis=-1),
        "wg": moe_w_gate.astype(BF16), "wu": moe_w_up.astype(BF16), "wd": moe_w_down.astype(BF16),
        "rope": {n: _rope_tables(n) for n in {x_prompt.shape[1], x_sample.shape[1]}},
    }
    return _trunk(x_prompt, p), _trunk(x_sample, p)
```
